```python
import math
import jax, jax.numpy as jnp
from jax import lax
import numpy as np

D_MODEL = 1024
BATCH = 8
SEQ = 2048
DEPTH = 1
DEC_BATCH = 128
DEC_SEQ = 8
PAST_LEN = 16384
PAGE_SIZE = 128

HG_HEADS = 4
HG_DK = 128
HG_DV = 128
HG_FDIM = HG_HEADS * HG_DK
HG_WIDTH = HG_HEADS * HG_DV
HG_CHUNK = 64
RW_HEADS = 8
RW_DH = 64
RW_WIDTH = RW_HEADS * RW_DH
LORA_W = 64
LORA_A = 64
LORA_G = 128
MIX_WIDTH = HG_WIDTH + RW_WIDTH
HG_COLS = 2 * HG_FDIM + 2 * HG_WIDTH
RW_COLS = 3 * RW_WIDTH + LORA_W + LORA_A + LORA_G
N_IN = HG_COLS + RW_COLS
N_EXPERTS = 64
TOP_K = 6
N_GROUPS = 8
TOPK_GROUPS = 4
D_EXPERT = 256
D_SHARED = 256
ROUTED_SCALE = 2.5
MOE_BLOCK = 512
ALPHA = (2.0 * DEPTH) ** 0.25
BETA = (8.0 * DEPTH) ** -0.25
LN_EPS = 1e-5
HEAD_NORM_EPS = 1e-6
RW_GN_EPS = 64e-5

kernel_name = "hgrn2_rwkv7_parallel_moe_decoder_step"


def layer_norm(x, g, b):
    xf = x.astype(jnp.float32)
    mu = jnp.mean(xf, axis=-1, keepdims=True)
    var = jnp.mean(jnp.square(xf - mu), axis=-1, keepdims=True)
    y = (xf - mu) * lax.rsqrt(var + LN_EPS) * g.astype(jnp.float32) + b.astype(jnp.float32)
    return y.astype(x.dtype)


def hgrn2_mixer(p, s0, lb_raw, layer, norm_g):
    B, T, _ = p.shape
    q, zf, v, g = jnp.split(p, [HG_FDIM, 2 * HG_FDIM, 2 * HG_FDIM + HG_WIDTH], axis=-1)
    lb = jnp.cumsum(jax.nn.softmax(lb_raw.astype(jnp.float32), axis=0), axis=0)[layer]
    f = lb + (1.0 - lb) * jax.nn.sigmoid(zf)
    log_f = jnp.log(f)
    k = 1.0 - f
    q = jax.nn.silu(q)
    C = math.gcd(T, HG_CHUNK)
    nC = T // C

    def to_chunks(a, d):
        return a.reshape(B, nC, C, HG_HEADS, d).transpose(1, 0, 3, 2, 4)

    mask = jnp.tril(jnp.ones((C, C), dtype=bool))[:, :, None]

    def step(S, inp):
        qc, kc, vc, lfc = inp
        b = jnp.cumsum(lfc, axis=2)
        o_inter = jnp.einsum('bhtk,bhkv->bhtv', qc * jnp.exp(b), S)
        diff = b[:, :, :, None, :] - b[:, :, None, :, :]
        dec = jnp.where(mask, jnp.exp(jnp.where(mask, diff, 0.0)), 0.0)
        att = jnp.einsum('bhtk,bhsk,bhtsk->bhts', qc, kc, dec)
        o_intra = jnp.einsum('bhts,bhsv->bhtv', att, vc)
        b_last = b[:, :, -1:, :]
        S_new = jnp.exp(b_last[:, :, 0, :])[..., None] * S + jnp.einsum('bhsk,bhsv->bhkv', kc * jnp.exp(b_last - b), vc)
        return S_new, o_inter + o_intra

    s_final, o = lax.scan(step, s0, (to_chunks(q, HG_DK), to_chunks(k, HG_DK), to_chunks(v, HG_DV), to_chunks(log_f, HG_DK)))
    o = o.transpose(1, 0, 3, 2, 4).reshape(B, T, HG_HEADS, HG_DV)
    o = o * lax.rsqrt(jnp.mean(jnp.square(o), axis=-1, keepdims=True) + HEAD_NORM_EPS) * norm_g.reshape(HG_HEADS, HG_DV)
    out = o.reshape(B, T, HG_WIDTH) * jax.nn.silu(g)
    return out, s_final


def rwkv7_mixer(p, shift_prev, s0, mu, w0, w2, a0, a2, g2, k_k, k_a, r_k, gn_g, gn_b):
    B, T, _ = p.shape
    prev = jnp.concatenate([shift_prev[:, None, :], p[:, :-1, :]], axis=1)
    ps = p + mu * (prev - p)
    r, k, v, zw, za, zg = jnp.split(ps, [RW_WIDTH, 2 * RW_WIDTH, 3 * RW_WIDTH, 3 * RW_WIDTH + LORA_W, 3 * RW_WIDTH + LORA_W + LORA_A], axis=-1)
    w = -jax.nn.softplus(-(w0 + jnp.tanh(zw) @ w2)) - 0.5
    decay = jnp.exp(-jnp.exp(w))
    a = jax.nn.sigmoid(a0 + za @ a2)
    g = jax.nn.sigmoid(zg) @ g2
    heads = lambda t: t.reshape(B, T, RW_HEADS, RW_DH)
    kk = heads(k * k_k)
    kk = kk / jnp.maximum(jnp.sqrt(jnp.sum(jnp.square(kk), axis=-1, keepdims=True)), 1e-12)
    k = k * (1.0 + (a - 1.0) * k_a)
    r_h, k_h, v_h, w_h, a_h = heads(r), heads(k), heads(v), heads(decay), heads(a)

    def step(S, inp):
        rt, kt, vt, wt, kkt, at = inp
        sa = jnp.einsum('bhvk,bhk->bhv', S, -kkt)
        S = S * wt[:, :, None, :] + sa[..., None] * (kkt * at)[:, :, None, :] + vt[..., None] * kt[:, :, None, :]
        return S, jnp.einsum('bhvk,bhk->bhv', S, rt)

    tm = lambda t: t.transpose(1, 0, 2, 3)
    s_final, o = lax.scan(step, s0, (tm(r_h), tm(k_h), tm(v_h), tm(w_h), tm(kk), tm(a_h)))
    o = tm(o)
    mean = jnp.mean(o, axis=-1, keepdims=True)
    var = jnp.mean(jnp.square(o - mean), axis=-1, keepdims=True)
    o = (o - mean) * lax.rsqrt(var + RW_GN_EPS) * gn_g.reshape(RW_HEADS, RW_DH) + gn_b.reshape(RW_HEADS, RW_DH)
    bonus = jnp.sum(r_h * k_h * r_k, axis=-1, keepdims=True) * v_h
    out = (o + bonus).reshape(B, T, RW_WIDTH) * g
    return out, s_final, p[:, -1, :]


def moe_ffn(h, w_router, e_bias, w_gate, w_up, w_down, ws_gate, ws_up, ws_down):
    B, T, D = h.shape
    n = B * T
    x = h.reshape(n, D)
    scores = jax.nn.sigmoid(x.astype(jnp.float32) @ w_router.astype(jnp.float32))
    choice = scores + e_bias.astype(jnp.float32)
    grp = choice.reshape(n, N_GROUPS, N_EXPERTS // N_GROUPS)
    grp_score = jnp.sum(lax.top_k(grp, 2)[0], axis=-1)
    _, g_idx = lax.top_k(grp_score, TOPK_GROUPS)
    g_mask = jnp.sum(jax.nn.one_hot(g_idx, N_GROUPS), axis=1) > 0
    e_mask = jnp.repeat(g_mask, N_EXPERTS // N_GROUPS, axis=1)
    _, e_idx = lax.top_k(jnp.where(e_mask, choice, -jnp.inf), TOP_K)
    wts = jnp.take_along_axis(scores, e_idx, axis=1)
    wts = wts / jnp.sum(wts, axis=1, keepdims=True) * ROUTED_SCALE
    gates = jnp.sum(jax.nn.one_hot(e_idx, N_EXPERTS, dtype=jnp.float32) * wts[..., None], axis=1)
    nb = -(-n // MOE_BLOCK)
    pad = nb * MOE_BLOCK - n
    xp = jnp.pad(x, ((0, pad), (0, 0))).reshape(nb, MOE_BLOCK, D)
    gp = jnp.pad(gates, ((0, pad), (0, 0))).reshape(nb, MOE_BLOCK, N_EXPERTS)

    def block(args):
        xb, gb = args
        hg = jnp.einsum('td,edf->tef', xb, w_gate)
        hu = jnp.einsum('td,edf->tef', xb, w_up)
        act = jax.nn.silu(hg) * hu * gb[..., None].astype(xb.dtype)
        return jnp.einsum('tef,efd->td', act, w_down)

    routed = lax.map(block, (xp, gp)).reshape(nb * MOE_BLOCK, D)[:n]
    shared = (jax.nn.silu(x @ ws_gate) * (x @ ws_up)) @ ws_down
    return (routed + shared).reshape(B, T, D)


def trunk_layer(x, s_hg, s_rw, shift, layer, lw):
    proj = (x @ lw['w_in']).astype(jnp.float32)
    hg_out, s_hg_new = hgrn2_mixer(proj[..., :HG_COLS], s_hg.astype(jnp.float32), lw['hg_lb'], layer, lw['hg_norm_g'].astype(jnp.float32))
    rw_out, s_rw_new, shift_new = rwkv7_mixer(
        proj[..., HG_COLS:], shift.astype(jnp.float32), s_rw.astype(jnp.float32),
        lw['rw_mu'].astype(jnp.float32), lw['rw_w0'].astype(jnp.float32), lw['rw_w2'].astype(jnp.float32),
        lw['rw_a0'].astype(jnp.float32), lw['rw_a2'].astype(jnp.float32), lw['rw_g2'].astype(jnp.float32),
        lw['rw_k_k'].astype(jnp.float32), lw['rw_k_a'].astype(jnp.float32), lw['rw_r_k'].astype(jnp.float32),
        lw['rw_gn_g'].astype(jnp.float32), lw['rw_gn_b'].astype(jnp.float32))
    mix = jnp.concatenate([hg_out, rw_out], axis=-1).astype(x.dtype) @ lw['w_out']
    h = layer_norm(ALPHA * x + mix, lw['ln1_g'], lw['ln1_b'])
    ffn = moe_ffn(h, lw['w_router'], lw['e_bias'], lw['w_gate'], lw['w_up'], lw['w_down'], lw['ws_gate'], lw['ws_up'], lw['ws_down'])
    y = layer_norm(ALPHA * h + ffn, lw['ln2_g'], lw['ln2_b'])
    return y, s_hg_new, s_rw_new, shift_new


def setup_inputs(seed: int = 0) -> dict:
    key = jax.random.key(seed)
    ks = jax.random.split(key, 32)
    nrm = lambda i, shape: jax.random.normal(ks[i], shape, dtype=jnp.float32)
    col_scale = jnp.ones((N_IN,), jnp.float32)
    col_scale = col_scale.at[2 * HG_FDIM:2 * HG_FDIM + HG_WIDTH].set(BETA)
    col_scale = col_scale.at[HG_COLS + 2 * RW_WIDTH:HG_COLS + 3 * RW_WIDTH].set(BETA)
    ratio = jnp.arange(RW_DH, dtype=jnp.float32) / (RW_DH - 1)
    w0_base = jnp.tile(-6.0 + 5.0 * ratio ** 0.7, RW_HEADS)
    return {
        "x_prompt": nrm(0, (BATCH, SEQ, D_MODEL)),
        "x_sample": nrm(1, (DEC_BATCH, DEC_SEQ, D_MODEL)),
        "state_hgrn": 0.3 * nrm(2, (DEPTH, DEC_BATCH, HG_HEADS, HG_DK, HG_DV)),
        "state_rwkv": 0.3 * nrm(3, (DEPTH, DEC_BATCH, RW_HEADS, RW_DH, RW_DH)),
        "state_shift": nrm(4, (DEPTH, DEC_BATCH, RW_COLS)),
        "w_in": nrm(5, (DEPTH, D_MODEL, N_IN)) * D_MODEL ** -0.5 * col_scale,
        "w_out": nrm(6, (DEPTH, MIX_WIDTH, D_MODEL)) * MIX_WIDTH ** -0.5 * BETA,
        "hg_lb": 0.5 * nrm(7, (DEPTH + 1, HG_FDIM)),
        "hg_norm_g": 1.0 + 0.02 * nrm(8, (DEPTH, HG_WIDTH)),
        "rw_mu": jax.random.uniform(ks[9], (DEPTH, RW_COLS), dtype=jnp.float32),
        "rw_w0": w0_base[None, :] + 0.1 * nrm(10, (DEPTH, RW_WIDTH)),
        "rw_w2": 0.1 * nrm(11, (DEPTH, LORA_W, RW_WIDTH)),
        "rw_a0": 0.1 * nrm(12, (DEPTH, RW_WIDTH)),
        "rw_a2": 0.5 * LORA_A ** -0.5 * nrm(13, (DEPTH, LORA_A, RW_WIDTH)),
        "rw_g2": LORA_G ** -0.5 * nrm(14, (DEPTH, LORA_G, RW_WIDTH)),
        "rw_k_k": 0.85 + 0.02 * nrm(15, (DEPTH, RW_WIDTH)),
        "rw_k_a": 1.0 + 0.02 * nrm(16, (DEPTH, RW_WIDTH)),
        "rw_r_k": 0.1 * nrm(17, (DEPTH, RW_HEADS, RW_DH)),
        "rw_gn_g": 1.0 + 0.02 * nrm(18, (DEPTH, RW_WIDTH)),
        "rw_gn_b": 0.02 * nrm(19, (DEPTH, RW_WIDTH)),
        "ln1_g": 1.0 + 0.02 * nrm(20, (DEPTH, D_MODEL)),
        "ln1_b": 0.02 * nrm(21, (DEPTH, D_MODEL)),
        "ln2_g": 1.0 + 0.02 * nrm(22, (DEPTH, D_MODEL)),
        "ln2_b": 0.02 * nrm(23, (DEPTH, D_MODEL)),
        "w_router": D_MODEL ** -0.5 * nrm(24, (DEPTH, D_MODEL, N_EXPERTS)),
        "e_bias": 0.01 * nrm(25, (DEPTH, N_EXPERTS)),
        "w_gate": D_MODEL ** -0.5 * nrm(26, (DEPTH, N_EXPERTS, D_MODEL, D_EXPERT)),
        "w_up": D_MODEL ** -0.5 * nrm(27, (DEPTH, N_EXPERTS, D_MODEL, D_EXPERT)),
        "w_down": D_EXPERT ** -0.5 * BETA * nrm(28, (DEPTH, N_EXPERTS, D_EXPERT, D_MODEL)),
        "ws_gate": D_MODEL ** -0.5 * nrm(29, (DEPTH, D_MODEL, D_SHARED)),
        "ws_up": D_MODEL ** -0.5 * nrm(30, (DEPTH, D_MODEL, D_SHARED)),
        "ws_down": D_SHARED ** -0.5 * BETA * nrm(31, (DEPTH, D_SHARED, D_MODEL)),
    }


def reference(x_prompt, x_sample, state_hgrn, state_rwkv, state_shift, w_in, w_out, hg_lb, hg_norm_g,
              rw_mu, rw_w0, rw_w2, rw_a0, rw_a2, rw_g2, rw_k_k, rw_k_a, rw_r_k, rw_gn_g, rw_gn_b,
              ln1_g, ln1_b, ln2_g, ln2_b, w_router, e_bias, w_gate, w_up, w_down, ws_gate, ws_up, ws_down):
    hp, hs = x_prompt, x_sample
    Bp = x_prompt.shape[0]
    hg_p, rw_p, sh_p, hg_s, rw_s, sh_s = [], [], [], [], [], []
    for l in range(DEPTH):
        lw = {
            'w_in': w_in[l], 'w_out': w_out[l], 'hg_lb': hg_lb, 'hg_norm_g': hg_norm_g[l],
            'rw_mu': rw_mu[l], 'rw_w0': rw_w0[l], 'rw_w2': rw_w2[l], 'rw_a0': rw_a0[l], 'rw_a2': rw_a2[l],
            'rw_g2': rw_g2[l], 'rw_k_k': rw_k_k[l], 'rw_k_a': rw_k_a[l], 'rw_r_k': rw_r_k[l],
            'rw_gn_g': rw_gn_g[l], 'rw_gn_b': rw_gn_b[l], 'ln1_g': ln1_g[l], 'ln1_b': ln1_b[l],
            'ln2_g': ln2_g[l], 'ln2_b': ln2_b[l], 'w_router': w_router[l], 'e_bias': e_bias[l],
            'w_gate': w_gate[l], 'w_up': w_up[l], 'w_down': w_down[l],
            'ws_gate': ws_gate[l], 'ws_up': ws_up[l], 'ws_down': ws_down[l],
        }
        zero_hg = jnp.zeros((Bp, HG_HEADS, HG_DK, HG_DV), jnp.float32)
        zero_rw = jnp.zeros((Bp, RW_HEADS, RW_DH, RW_DH), jnp.float32)
        zero_sh = jnp.zeros((Bp, RW_COLS), jnp.float32)
        hp, a1, a2, a3 = trunk_layer(hp, zero_hg, zero_rw, zero_sh, l, lw)
        hs, b1, b2, b3 = trunk_layer(hs, state_hgrn[l], state_rwkv[l], state_shift[l], l, lw)
        hg_p.append(a1); rw_p.append(a2); sh_p.append(a3)
        hg_s.append(b1); rw_s.append(b2); sh_s.append(b3)
    return (hp, hs, jnp.stack(hg_p), jnp.stack(rw_p), jnp.stack(sh_p), jnp.stack(hg_s), jnp.stack(rw_s), jnp.stack(sh_s))
```

```python
import functools

import jax
import jax.numpy as jnp
from jax import lax
from jax.experimental import pallas as pl
from jax.experimental.pallas import tpu as pltpu

F32 = jnp.float32
BF16 = jnp.bfloat16

D_MODEL = 1024
HG_HEADS = 4
HG_DK = 128
HG_WIDTH = 512
HG_COLS = 2048
RW_HEADS = 8
RW_DH = 64
RW_WIDTH = 512
RW_COLS = 1792
N_EXPERTS = 64
TOP_K = 6
N_GROUPS = 8
TOPK_GROUPS = 4
GROUP_SIZE = N_EXPERTS // N_GROUPS
D_EXPERT = 256
ROUTED_SCALE = 2.5
DEPTH = 1
ALPHA = (2.0 * DEPTH) ** 0.25
LN_EPS = 1e-5
HEAD_NORM_EPS = 1e-6
RW_GN_EPS = 64e-5

LANES = 128
VMEM_LIMIT = 56 * 1024 * 1024


def _bf(x):
    return x.astype(BF16)


def _dot(a, b):
    return jnp.dot(a, b, preferred_element_type=F32)


def _dot_nt(a, b):
    return lax.dot_general(a, b, (((1,), (1,)), ((), ())), preferred_element_type=F32)


def _split2(x):
    hi = _bf(x)
    lo = _bf(x - hi.astype(F32))
    return hi, lo


def _split3(x):
    hi = _bf(x)
    r1 = x - hi.astype(F32)
    mid = _bf(r1)
    lo = _bf(r1 - mid.astype(F32))
    return hi, mid, lo


def _sigmoid(x):
    return 1.0 / (1.0 + jnp.exp(-x))


def _silu(x):
    return x * _sigmoid(x)


def _params(sem):
    return pltpu.CompilerParams(dimension_semantics=sem, vmem_limit_bytes=VMEM_LIMIT)


def _proj_kernel(x_ref, w_ref, hg_ref, rw_ref):
    xb = _bf(x_ref[...])
    step = 256
    for c0 in range(0, HG_COLS, step):
        hg_ref[:, c0:c0 + step] = _dot(xb, w_ref[:, c0:c0 + step])
    for c0 in range(0, RW_COLS, step):
        rw_ref[:, c0:c0 + step] = _dot(xb, w_ref[:, HG_COLS + c0:HG_COLS + c0 + step])


def _proj(x2d, w_in_bf, tm):
    n = x2d.shape[0]
    return pl.pallas_call(
        _proj_kernel,
        out_shape=(jax.ShapeDtypeStruct((n, HG_COLS), F32), jax.ShapeDtypeStruct((n, RW_COLS), F32)),
        grid=(n // tm,),
        in_specs=[pl.BlockSpec((tm, D_MODEL), lambda i: (i, 0)),
                  pl.BlockSpec((D_MODEL, HG_COLS + RW_COLS), lambda i: (0, 0))],
        out_specs=(pl.BlockSpec((tm, HG_COLS), lambda i: (i, 0)),
                   pl.BlockSpec((tm, RW_COLS), lambda i: (i, 0))),
        compiler_params=_params(("parallel",)),
        name="in_proj",
    )(x2d, w_in_bf)


def _hgrn_kernel(p_ref, s0_ref, lbraw_ref, ng_ref, out_ref, sfin_ref, st_ref, *, Bb, Tb, C, R):
    ti = pl.program_id(1)
    nt = pl.num_programs(1)
    nb = C // R

    @pl.when(ti == 0)
    def _():
        for b in range(Bb):
            for h in range(HG_HEADS):
                st_ref[b, h] = s0_ref[b, h].T

    lbr = lbraw_ref[...]
    lbe = jnp.exp(lbr - jnp.max(lbr, axis=0, keepdims=True))
    lb = lbe[0:1, :] / jnp.sum(lbe, axis=0, keepdims=True)
    ng = ng_ref[...]

    if Tb > 8:
        rows = lax.broadcasted_iota(jnp.int32, (Tb, Tb), 0)
        cols = lax.broadcasted_iota(jnp.int32, (Tb, Tb), 1)
        tril = _bf(jnp.where(((rows // C) == (cols // C)) & (cols <= rows), 1.0, 0.0))
    tidx = lax.broadcasted_iota(jnp.int32, (R, 1), 0)
    cidx = lax.broadcasted_iota(jnp.int32, (C, 1), 0)

    for b in range(Bb):
        qs = _silu(p_ref[b, :, 0:512])
        f = lb + (1.0 - lb) * _sigmoid(p_ref[b, :, 512:1024])
        lf = jnp.log(f)
        kd = 1.0 - f
        v = p_ref[b, :, 1024:1536]
        if Tb == 8:
            r8 = lax.broadcasted_iota(jnp.int32, (8, 1), 0)
            bcum = jnp.concatenate(
                [jnp.sum(jnp.where(r8 <= t, lf, 0.0), axis=0, keepdims=True) for t in range(8)], axis=0)
        else:
            hi, mid, lo = _split3(lf)
            bcum = _dot(tril, hi) + _dot(tril, mid) + _dot(tril, lo)
        o_heads = []
        for h in range(HG_HEADS):
            sl = slice(h * HG_DK, (h + 1) * HG_DK)
            st = st_ref[b, h]
            o_chunks = []
            for c in range(Tb // C):
                r0 = c * C
                bc = bcum[r0:r0 + C, sl]
                qc = qs[r0:r0 + C, sl]
                kc = kd[r0:r0 + C, sl]
                vc = v[r0:r0 + C, sl]
                b_last = bc[C - 1:C, :]
                o_inter = _dot_nt(_bf(qc * jnp.exp(bc)), _bf(st))
                o_blocks = []
                for i in range(nb):
                    rs = slice(i * R, (i + 1) * R)
                    bq = bc[rs]
                    qq = qc[rs]
                    acc = jnp.zeros((R, HG_DK), F32)
                    if i > 0:
                        anchor = bc[i * R - 1:i * R, :]
                        early = cidx < i * R
                        qa = qq * jnp.exp(bq - anchor)
                        ka = jnp.where(early, kc * jnp.exp(jnp.where(early, anchor - bc, 0.0)), 0.0)
                        att = _dot_nt(_bf(qa), _bf(ka))
                        acc = acc + _dot(_bf(att), _bf(vc))
                    for s in range(R):
                        sa = i * R + s
                        msk = tidx >= s
                        e = jnp.exp(jnp.where(msk, bq - bc[sa:sa + 1, :], 0.0))
                        term = jnp.where(msk, qq * kc[sa:sa + 1, :] * e, 0.0)
                        acc = acc + jnp.sum(term, axis=-1, keepdims=True) * vc[sa:sa + 1, :]
                    o_blocks.append(acc)
                o_intra = o_blocks[0] if nb == 1 else jnp.concatenate(o_blocks, axis=0)
                o_chunks.append(o_inter + o_intra)
                k_out = kc * jnp.exp(b_last - bc)
                st = st * jnp.exp(b_last) + _dot(_bf(vc.T), _bf(k_out))
            st_ref[b, h] = st
            o_h = o_chunks[0] if len(o_chunks) == 1 else jnp.concatenate(o_chunks, axis=0)
            o_h = o_h * lax.rsqrt(jnp.mean(o_h * o_h, axis=-1, keepdims=True) + HEAD_NORM_EPS) * ng[:, sl]
            o_heads.append(o_h)
        out_ref[b] = jnp.concatenate(o_heads, axis=-1) * _silu(p_ref[b, :, 1536:2048])

    @pl.when(ti == nt - 1)
    def _():
        for b in range(Bb):
            for h in range(HG_HEADS):
                sfin_ref[b, h] = st_ref[b, h].T


def _hgrn(p_hg, s0, lb_raw, norm_g, *, Bb, Tb, C, R):
    B, T, _ = p_hg.shape
    kern = functools.partial(_hgrn_kernel, Bb=Bb, Tb=Tb, C=C, R=R)
    return pl.pallas_call(
        kern,
        out_shape=(jax.ShapeDtypeStruct((B, T, HG_WIDTH), F32),
                   jax.ShapeDtypeStruct((B, HG_HEADS, HG_DK, HG_DK), F32)),
        grid=(B // Bb, T // Tb),
        in_specs=[pl.BlockSpec((Bb, Tb, HG_COLS), lambda i, j: (i, j, 0)),
                  pl.BlockSpec((Bb, HG_HEADS, HG_DK, HG_DK), lambda i, j: (i, 0, 0, 0)),
                  pl.BlockSpec((DEPTH + 1, HG_WIDTH), lambda i, j: (0, 0)),
                  pl.BlockSpec((1, HG_WIDTH), lambda i, j: (0, 0))],
        out_specs=(pl.BlockSpec((Bb, Tb, HG_WIDTH), lambda i, j: (i, j, 0)),
                   pl.BlockSpec((Bb, HG_HEADS, HG_DK, HG_DK), lambda i, j: (i, 0, 0, 0))),
        scratch_shapes=[pltpu.VMEM((Bb, HG_HEADS, HG_DK, HG_DK), F32)],
        compiler_params=_params(("parallel", "arbitrary")),
        name="hgrn2_mixer",
    )(p_hg, s0, lb_raw, norm_g)


RW_PAIRS = RW_HEADS // 2
RW_SEQ = 8


def _block_ones():
    i = lax.broadcasted_iota(jnp.int32, (LANES, LANES), 0)
    j = lax.broadcasted_iota(jnp.int32, (LANES, LANES), 1)
    return _bf(jnp.where((i // RW_DH) == (j // RW_DH), 1.0, 0.0))


def _seg_sum(x, bd):
    outs = []
    for c in range(RW_WIDTH // LANES):
        hi, lo = _split2(x[:, c * LANES:(c + 1) * LANES])
        outs.append(_dot(hi, bd) + _dot(lo, bd))
    return jnp.concatenate(outs, axis=-1)


def _rwkv_kernel(p_ref, shift_ref, s0_ref, mu_ref, w0_ref, wa_ref, a0_ref, g2_ref, kk_ref, ka_ref,
                 rk_ref, gng_ref, gnb_ref, out_ref, sfin_ref,
                 s_ref, prev_ref, r_s, w_s, k_s, v_s, kk_s, kka_s, g_s, o_s, *, tb):
    ti = pl.program_id(1)
    nt = pl.num_programs(1)

    @pl.when(ti == 0)
    def _():
        prev_ref[...] = shift_ref[...]
        for b in range(RW_SEQ):
            for p in range(RW_PAIRS):
                s_ref[b, p] = jnp.concatenate([s0_ref[b, 2 * p], s0_ref[b, 2 * p + 1]], axis=-1)

    bd = _block_ones()
    lane = lax.broadcasted_iota(jnp.int32, (1, LANES), 1)
    rid = lax.broadcasted_iota(jnp.int32, (tb, 1), 0)
    mu = mu_ref[...]

    for b in range(RW_SEQ):
        p = p_ref[b]
        prev = jnp.where(rid == 0, prev_ref[b:b + 1, :], pltpu.roll(p, 1, axis=0))
        prev_ref[b:b + 1, :] = p[tb - 1:tb, :]
        ps = p + mu * (prev - p)
        r = ps[:, 0:512]
        k = ps[:, 512:1024]
        v = ps[:, 1024:1536]
        zz = ps[:, 1536:1664]
        zg = ps[:, 1664:1792]
        lora = _dot(_bf(jnp.where(lane < 64, jnp.tanh(zz), zz)), wa_ref[...])
        wl = -(w0_ref[...] + lora[:, 0:512])
        w_raw = -(jnp.maximum(wl, 0.0) + jnp.log(1.0 + jnp.exp(-jnp.abs(wl)))) - 0.5
        a = _sigmoid(a0_ref[...] + lora[:, 512:1024])
        kk = k * kk_ref[...]
        kk = kk / jnp.maximum(jnp.sqrt(_seg_sum(kk * kk, bd)), 1e-12)
        r_s[b] = r
        w_s[b] = jnp.exp(-jnp.exp(w_raw))
        k_s[b] = k * (1.0 + (a - 1.0) * ka_ref[...])
        v_s[b] = v
        kk_s[b] = kk
        kka_s[b] = kk * a
        g_s[b] = _dot(_bf(_sigmoid(zg)), g2_ref[...])

    vi = lax.broadcasted_iota(jnp.int32, (RW_DH, LANES), 0)
    li = lax.broadcasted_iota(jnp.int32, (RW_DH, LANES), 1)
    diag = (li % RW_DH) == vi
    pairs = range(RW_PAIRS)

    sub = 8

    def steps(tq, carry):
        t0 = pl.multiple_of(tq * sub, sub)
        for b in range(RW_SEQ):
            tile = lambda ref: [ref[b, pl.ds(t0, sub), pl.ds(p * LANES, LANES)] for p in pairs]
            kk_t, w_t, kka_t, k_t, r_t, v_t = (tile(ref) for ref in (kk_s, w_s, kka_s, k_s, r_s, v_s))
            s = [s_ref[b, p] for p in pairs]
            o_rows = [[] for _ in pairs]
            for j in range(sub):
                row = lambda tl, p: tl[p][j:j + 1, :]
                x_sa = jnp.concatenate([s[p] * row(kk_t, p) for p in pairs], axis=0)
                x_v = jnp.concatenate([jnp.where(diag, row(v_t, p), 0.0) for p in pairs], axis=0)
                sa = _dot(_bf(x_sa), bd)
                vh, vl = _split2(x_v)
                vb = _dot(vh, bd) + _dot(vl, bd)
                s = [s[p] * row(w_t, p) - sa[p * RW_DH:(p + 1) * RW_DH] * row(kka_t, p)
                     + vb[p * RW_DH:(p + 1) * RW_DH] * row(k_t, p) for p in pairs]
                x_o = jnp.concatenate([s[p] * row(r_t, p) for p in pairs], axis=0)
                ob = _dot(_bf(x_o), bd)
                for p in pairs:
                    o_rows[p].append(jnp.sum(jnp.where(diag, ob[p * RW_DH:(p + 1) * RW_DH], 0.0),
                                             axis=0, keepdims=True))
            for p in pairs:
                s_ref[b, p] = s[p]
                o_s[b, pl.ds(t0, sub), pl.ds(p * LANES, LANES)] = jnp.concatenate(o_rows[p], axis=0)
        return carry

    lax.fori_loop(0, tb // sub, steps, 0)

    for b in range(RW_SEQ):
        o = o_s[b]
        d = o - _seg_sum(o, bd) * (1.0 / RW_DH)
        var = _seg_sum(d * d, bd) * (1.0 / RW_DH)
        on = d * lax.rsqrt(var + RW_GN_EPS) * gng_ref[...] + gnb_ref[...]
        bonus = _seg_sum(r_s[b] * k_s[b] * rk_ref[...], bd) * v_s[b]
        out_ref[b] = (on + bonus) * g_s[b]

    @pl.when(ti == nt - 1)
    def _():
        for b in range(RW_SEQ):
            for p in range(RW_PAIRS):
                s = s_ref[b, p]
                sfin_ref[b, 2 * p] = s[:, 0:RW_DH]
                sfin_ref[b, 2 * p + 1] = s[:, RW_DH:LANES]


def _rwkv(p_rw, shift, s0, mu, w0, wa, a0, g2, k_k, k_a, r_k, gn_g, gn_b, *, tb):
    B, T, _ = p_rw.shape
    kern = functools.partial(_rwkv_kernel, tb=tb)
    vec = lambda n: pl.BlockSpec((1, n), lambda i, j: (0, 0))
    blk = pltpu.VMEM((RW_SEQ, tb, RW_WIDTH), F32)
    return pl.pallas_call(
        kern,
        out_shape=(jax.ShapeDtypeStruct((B, T, RW_WIDTH), F32),
                   jax.ShapeDtypeStruct((B, RW_HEADS, RW_DH, RW_DH), F32)),
        grid=(B // RW_SEQ, T // tb),
        in_specs=[pl.BlockSpec((RW_SEQ, tb, RW_COLS), lambda i, j: (i, j, 0)),
                  pl.BlockSpec((RW_SEQ, RW_COLS), lambda i, j: (i, 0)),
                  pl.BlockSpec((RW_SEQ, RW_HEADS, RW_DH, RW_DH), lambda i, j: (i, 0, 0, 0)),
                  vec(RW_COLS), vec(RW_WIDTH),
                  pl.BlockSpec((LANES, 2 * RW_WIDTH), lambda i, j: (0, 0)),
                  vec(RW_WIDTH),
                  pl.BlockSpec((LANES, RW_WIDTH), lambda i, j: (0, 0)),
                  vec(RW_WIDTH), vec(RW_WIDTH), vec(RW_WIDTH), vec(RW_WIDTH), vec(RW_WIDTH)],
        out_specs=(pl.BlockSpec((RW_SEQ, tb, RW_WIDTH), lambda i, j: (i, j, 0)),
                   pl.BlockSpec((RW_SEQ, RW_HEADS, RW_DH, RW_DH), lambda i, j: (i, 0, 0, 0))),
        scratch_shapes=[pltpu.VMEM((RW_SEQ, RW_PAIRS, RW_DH, LANES), F32),
                        pltpu.VMEM((RW_SEQ, RW_COLS), F32),
                        blk, blk, blk, blk, blk, blk, blk, blk],
        compiler_params=_params(("parallel", "arbitrary")),
        name="rwkv7_mixer",
    )(p_rw, shift, s0, mu, w0, wa, a0, g2, k_k, k_a, r_k, gn_g, gn_b)


def _layer_norm(y, g, b):
    mu = jnp.mean(y, axis=-1, keepdims=True)
    d = y - mu
    var = jnp.mean(d * d, axis=-1, keepdims=True)
    return d * lax.rsqrt(var + LN_EPS) * g + b


def _outproj_kernel(x_ref, hg_ref, rw_ref, w_ref, g_ref, b_ref, h_ref):
    mix = (_dot(_bf(hg_ref[...]), w_ref[0:HG_WIDTH, :])
           + _dot(_bf(rw_ref[...]), w_ref[HG_WIDTH:HG_WIDTH + RW_WIDTH, :]))
    h_ref[...] = _layer_norm(ALPHA * x_ref[...] + mix, g_ref[...], b_ref[...])


def _outproj(x2d, hg2d, rw2d, w_out_bf, ln_g, ln_b, tm):
    n = x2d.shape[0]
    rows = lambda w: pl.BlockSpec((tm, w), lambda i: (i, 0))
    return pl.pallas_call(
        _outproj_kernel,
        out_shape=jax.ShapeDtypeStruct((n, D_MODEL), F32),
        grid=(n // tm,),
        in_specs=[rows(D_MODEL), rows(HG_WIDTH), rows(RW_WIDTH),
                  pl.BlockSpec((D_MODEL, D_MODEL), lambda i: (0, 0)),
                  pl.BlockSpec((1, D_MODEL), lambda i: (0, 0)),
                  pl.BlockSpec((1, D_MODEL), lambda i: (0, 0))],
        out_specs=rows(D_MODEL),
        compiler_params=_params(("parallel",)),
        name="out_proj_ln",
    )(x2d, hg2d, rw2d, w_out_bf, ln_g, ln_b)


def _router_kernel(h_ref, w_ref, eb_ref, gates_ref):
    tm = h_ref.shape[0]
    hh, hl = _split2(h_ref[...])
    wh, wl = _split2(w_ref[...])
    logits = _dot(hh, wh) + (_dot(hh, wl) + _dot(hl, wh))
    lane = lax.broadcasted_iota(jnp.int32, (tm, LANES), 1)
    valid = lane < N_EXPERTS
    neg = -jnp.inf
    scores = _sigmoid(logits)
    choice = jnp.where(valid, scores + eb_ref[...], neg)

    def partner(x, s):
        return jnp.where((lane & s) != 0, pltpu.roll(x, s, axis=1), pltpu.roll(x, LANES - s, axis=1))

    def group_max(x):
        for s in (1, 2, 4):
            x = jnp.maximum(x, partner(x, s))
        return x

    def group_min(x):
        for s in (1, 2, 4):
            x = jnp.minimum(x, partner(x, s))
        return x

    m1 = group_max(choice)
    first = group_min(jnp.where(choice == m1, lane, LANES))
    m2 = group_max(jnp.where(lane == first, neg, choice))
    gscore = m1 + m2
    gper = jnp.where(valid, gscore, pltpu.roll(gscore, N_EXPERTS, axis=1))
    gidx = (lane % N_EXPERTS) // GROUP_SIZE
    rank = jnp.zeros((tm, LANES), jnp.int32)
    for j in range(1, N_GROUPS):
        other = pltpu.roll(gper, GROUP_SIZE * j, axis=1)
        beats = (other > gper) | ((other == gper) & (gidx >= j))
        rank = rank + beats.astype(jnp.int32)
    cand = jnp.where(valid & (rank < TOPK_GROUPS), choice, neg)
    sel = jnp.zeros((tm, LANES), jnp.bool_)
    for _ in range(TOP_K):
        m = jnp.max(cand, axis=-1, keepdims=True)
        idx = jnp.min(jnp.where(cand == m, lane, LANES), axis=-1, keepdims=True)
        hit = lane == idx
        sel = sel | hit
        cand = jnp.where(hit, neg, cand)
    wts = jnp.where(sel, scores, 0.0)
    gates = wts / jnp.sum(wts, axis=-1, keepdims=True) * ROUTED_SCALE
    gates_ref[...] = gates[:, 0:N_EXPERTS]


def _router(h2d, w_router_pad, e_bias_pad, tm):
    n = h2d.shape[0]
    return pl.pallas_call(
        _router_kernel,
        out_shape=jax.ShapeDtypeStruct((n, N_EXPERTS), F32),
        grid=(n // tm,),
        in_specs=[pl.BlockSpec((tm, D_MODEL), lambda i: (i, 0)),
                  pl.BlockSpec((D_MODEL, LANES), lambda i: (0, 0)),
                  pl.BlockSpec((1, LANES), lambda i: (0, 0))],
        out_specs=pl.BlockSpec((tm, N_EXPERTS), lambda i: (i, 0)),
        compiler_params=_params(("parallel",)),
        name="router",
    )(h2d, w_router_pad, e_bias_pad)


def _moe_kernel(h_ref, gate_ref, wg_ref, wu_ref, wd_ref, sg_ref, su_ref, sd_ref, g_ref, b_ref,
                out_ref, acc_ref, xb_ref):
    e = pl.program_id(1)

    @pl.when(e == 0)
    def _():
        xb = _bf(h_ref[...])
        xb_ref[...] = xb
        act = _silu(_dot(xb, sg_ref[...])) * _dot(xb, su_ref[...])
        acc_ref[...] = _dot(_bf(act), sd_ref[...])

    xb = xb_ref[...]
    hg = _dot(xb, _bf(wg_ref[0]))
    hu = _dot(xb, _bf(wu_ref[0]))
    act = _silu(hg) * hu * gate_ref[0]
    acc_ref[...] += _dot(_bf(act), _bf(wd_ref[0]))

    @pl.when(e == pl.num_programs(1) - 1)
    def _():
        out_ref[...] = _layer_norm(ALPHA * h_ref[...] + acc_ref[...], g_ref[...], b_ref[...])


def _moe(h2d, gates_t, w_gate, w_up, w_down, sg_bf, su_bf, sd_bf, ln_g, ln_b, tm):
    n = h2d.shape[0]
    const = lambda shape: pl.BlockSpec(shape, lambda i, e: (0,) * len(shape))
    return pl.pallas_call(
        _moe_kernel,
        out_shape=jax.ShapeDtypeStruct((n, D_MODEL), F32),
        grid=(n // tm, N_EXPERTS),
        in_specs=[pl.BlockSpec((tm, D_MODEL), lambda i, e: (i, 0)),
                  pl.BlockSpec((1, tm, 1), lambda i, e: (e, i, 0)),
                  pl.BlockSpec((1, D_MODEL, D_EXPERT), lambda i, e: (e, 0, 0)),
                  pl.BlockSpec((1, D_MODEL, D_EXPERT), lambda i, e: (e, 0, 0)),
                  pl.BlockSpec((1, D_EXPERT, D_MODEL), lambda i, e: (e, 0, 0)),
                  const((D_MODEL, D_EXPERT)), const((D_MODEL, D_EXPERT)), const((D_EXPERT, D_MODEL)),
                  const((1, D_MODEL)), const((1, D_MODEL))],
        out_specs=pl.BlockSpec((tm, D_MODEL), lambda i, e: (i, 0)),
        scratch_shapes=[pltpu.VMEM((tm, D_MODEL), F32), pltpu.VMEM((tm, D_MODEL), BF16)],
        compiler_params=_params(("parallel", "arbitrary")),
        name="moe_experts",
    )(h2d, gates_t, w_gate, w_up, w_down, sg_bf, su_bf, sd_bf, ln_g, ln_b)


def _row_tile(n, cap):
    t = cap
    while n % t:
        t //= 2
    return t


def _trunk(x, s_hg, s_rw, shift, w, *, hg_cfg, rw_tb):
    B, T, _ = x.shape
    n = B * T
    x2d = x.reshape(n, D_MODEL)
    p_hg, p_rw = _proj(x2d, w["w_in"], _row_tile(n, 512))
    p_hg = p_hg.reshape(B, T, HG_COLS)
    p_rw = p_rw.reshape(B, T, RW_COLS)
    hg_out, hg_state = _hgrn(p_hg, s_hg, w["hg_lb"], w["hg_norm_g"], **hg_cfg)
    rw_out, rw_state = _rwkv(p_rw, shift, s_rw, w["rw_mu"], w["rw_w0"], w["rw_wa"], w["rw_a0"],
                             w["rw_g2"], w["rw_k_k"], w["rw_k_a"], w["rw_r_k"], w["rw_gn_g"],
                             w["rw_gn_b"], tb=rw_tb)
    h = _outproj(x2d, hg_out.reshape(n, HG_WIDTH), rw_out.reshape(n, RW_WIDTH), w["w_out"],
                 w["ln1_g"], w["ln1_b"], _row_tile(n, 512))
    gates = _router(h, w["w_router"], w["e_bias"], _row_tile(n, 512))
    gates_t = gates.T.reshape(N_EXPERTS, n, 1)
    y = _moe(h, gates_t, w["w_gate"], w["w_up"], w["w_down"], w["ws_gate"], w["ws_up"], w["ws_down"],
             w["ln2_g"], w["ln2_b"], _row_tile(n, 1024))
    return y.reshape(B, T, D_MODEL), hg_state, rw_state, p_rw[:, T - 1, :]


def kernel(x_prompt, x_sample, state_hgrn, state_rwkv, state_shift, w_in, w_out, hg_lb, hg_norm_g, rw_mu, rw_w0, rw_w2, rw_a0, rw_a2, rw_g2, rw_k_k, rw_k_a, rw_r_k, rw_gn_g, rw_gn_b, ln1_g, ln1_b, ln2_g, ln2_b, w_router, e_bias, w_gate, w_up, w_down, ws_gate, ws_up, ws_down):
    l = 0
    zeros_lora = jnp.zeros((RW_DH, RW_WIDTH), F32)
    w = {
        "w_in": _bf(w_in[l]), "w_out": _bf(w_out[l]),
        "hg_lb": hg_lb, "hg_norm_g": hg_norm_g[l][None, :],
        "rw_mu": rw_mu[l][None, :], "rw_w0": rw_w0[l][None, :], "rw_a0": rw_a0[l][None, :],
        "rw_wa": _bf(jnp.concatenate([jnp.concatenate([rw_w2[l], zeros_lora], axis=1),
                                      jnp.concatenate([zeros_lora, rw_a2[l]], axis=1)], axis=0)),
        "rw_g2": _bf(rw_g2[l]),
        "rw_k_k": rw_k_k[l][None, :], "rw_k_a": rw_k_a[l][None, :],
        "rw_r_k": rw_r_k[l].reshape(1, RW_WIDTH),
        "rw_gn_g": rw_gn_g[l][None, :], "rw_gn_b": rw_gn_b[l][None, :],
        "ln1_g": ln1_g[l][None, :], "ln1_b": ln1_b[l][None, :],
        "ln2_g": ln2_g[l][None, :], "ln2_b": ln2_b[l][None, :],
        "w_router": jnp.pad(w_router[l], ((0, 0), (0, LANES - N_EXPERTS))),
        "e_bias": jnp.pad(e_bias[l], (0, LANES - N_EXPERTS))[None, :],
        "w_gate": w_gate[l], "w_up": w_up[l], "w_down": w_down[l],
        "ws_gate": _bf(ws_gate[l]), "ws_up": _bf(ws_up[l]), "ws_down": _bf(ws_down[l]),
    }
    Bp = x_prompt.shape[0]
    zero_hg = jnp.zeros((Bp, HG_HEADS, HG_DK, HG_DK), F32)
    zero_rw = jnp.zeros((Bp, RW_HEADS, RW_DH, RW_DH), F32)
    zero_sh = jnp.zeros((Bp, RW_COLS), F32)
    yp, hg_p, rw_p, sh_p = _trunk(x_prompt, zero_hg, zero_rw, zero_sh, w,
                                  hg_cfg=dict(Bb=1, Tb=128, C=64, R=16), rw_tb=64)
    ys, hg_s, rw_s, sh_s = _trunk(x_sample, state_hgrn[l], state_rwkv[l], state_shift[l], w,
                                  hg_cfg=dict(Bb=8, Tb=8, C=8, R=8), rw_tb=8)
    return (yp, ys, hg_p[None], rw_p[None], sh_p[None], hg_s[None], rw_s[None], sh_s[None])
```

```python
import functools

import jax
import jax.numpy as jnp
from jax import lax
from jax.experimental import pallas as pl
from jax.experimental.pallas import tpu as pltpu

F32 = jnp.float32
BF16 = jnp.bfloat16

D_MODEL = 1024
HG_HEADS = 4
HG_DK = 128
HG_WIDTH = 512
HG_COLS = 2048
RW_HEADS = 8
RW_DH = 64
RW_WIDTH = 512
RW_COLS = 1792
N_EXPERTS = 64
TOP_K = 6
N_GROUPS = 8
TOPK_GROUPS = 4
GROUP_SIZE = N_EXPERTS // N_GROUPS
D_EXPERT = 256
ROUTED_SCALE = 2.5
DEPTH = 1
ALPHA = (2.0 * DEPTH) ** 0.25
LN_EPS = 1e-5
HEAD_NORM_EPS = 1e-6
RW_GN_EPS = 64e-5

LANES = 128
VMEM_LIMIT = 56 * 1024 * 1024


def _bf(x):
    return x.astype(BF16)


def _dot(a, b):
    return jnp.dot(a, b, preferred_element_type=F32)


def _dot_nt(a, b):
    return lax.dot_general(a, b, (((1,), (1,)), ((), ())), preferred_element_type=F32)


def _split2(x):
    hi = _bf(x)
    lo = _bf(x - hi.astype(F32))
    return hi, lo


def _split3(x):
    hi = _bf(x)
    r1 = x - hi.astype(F32)
    mid = _bf(r1)
    lo = _bf(r1 - mid.astype(F32))
    return hi, mid, lo


def _sigmoid(x):
    return 1.0 / (1.0 + jnp.exp(-x))


def _silu(x):
    return x * _sigmoid(x)


def _params(sem):
    return pltpu.CompilerParams(dimension_semantics=sem, vmem_limit_bytes=VMEM_LIMIT)


def _proj_kernel(x_ref, w_ref, hg_ref, rw_ref):
    xb = _bf(x_ref[...])
    step = 256
    for c0 in range(0, HG_COLS, step):
        hg_ref[:, c0:c0 + step] = _dot(xb, w_ref[:, c0:c0 + step])
    for c0 in range(0, RW_COLS, step):
        rw_ref[:, c0:c0 + step] = _dot(xb, w_ref[:, HG_COLS + c0:HG_COLS + c0 + step])


def _proj(x2d, w_in_bf, tm):
    n = x2d.shape[0]
    return pl.pallas_call(
        _proj_kernel,
        out_shape=(jax.ShapeDtypeStruct((n, HG_COLS), F32), jax.ShapeDtypeStruct((n, RW_COLS), F32)),
        grid=(n // tm,),
        in_specs=[pl.BlockSpec((tm, D_MODEL), lambda i: (i, 0)),
                  pl.BlockSpec((D_MODEL, HG_COLS + RW_COLS), lambda i: (0, 0))],
        out_specs=(pl.BlockSpec((tm, HG_COLS), lambda i: (i, 0)),
                   pl.BlockSpec((tm, RW_COLS), lambda i: (i, 0))),
        compiler_params=_params(("parallel",)),
        name="in_proj",
    )(x2d, w_in_bf)


def _hgrn_kernel(p_ref, s0_ref, lbraw_ref, ng_ref, out_ref, sfin_ref, st_ref, *, Bb, Tb, C, R):
    ti = pl.program_id(1)
    nt = pl.num_programs(1)
    nb = C // R

    @pl.when(ti == 0)
    def _():
        for b in range(Bb):
            for h in range(HG_HEADS):
                st_ref[b, h] = s0_ref[b, h].T

    lbr = lbraw_ref[...]
    lbe = jnp.exp(lbr - jnp.max(lbr, axis=0, keepdims=True))
    lb = lbe[0:1, :] / jnp.sum(lbe, axis=0, keepdims=True)
    ng = ng_ref[...]

    if Tb > 8:
        rows = lax.broadcasted_iota(jnp.int32, (Tb, Tb), 0)
        cols = lax.broadcasted_iota(jnp.int32, (Tb, Tb), 1)
        tril = _bf(jnp.where(((rows // C) == (cols // C)) & (cols <= rows), 1.0, 0.0))
    tidx = lax.broadcasted_iota(jnp.int32, (R, 1), 0)
    cidx = lax.broadcasted_iota(jnp.int32, (C, 1), 0)

    for b in range(Bb):
        qs = _silu(p_ref[b, :, 0:512])
        f = lb + (1.0 - lb) * _sigmoid(p_ref[b, :, 512:1024])
        lf = jnp.log(f)
        kd = 1.0 - f
        v = p_ref[b, :, 1024:1536]
        if Tb == 8:
            r8 = lax.broadcasted_iota(jnp.int32, (8, 1), 0)
            bcum = jnp.concatenate(
                [jnp.sum(jnp.where(r8 <= t, lf, 0.0), axis=0, keepdims=True) for t in range(8)], axis=0)
        else:
            hi, mid, lo = _split3(lf)
            bcum = _dot(tril, hi) + _dot(tril, mid) + _dot(tril, lo)
        o_heads = []
        for h in range(HG_HEADS):
            sl = slice(h * HG_DK, (h + 1) * HG_DK)
            st = st_ref[b, h]
            o_chunks = []
            for c in range(Tb // C):
                r0 = c * C
                bc = bcum[r0:r0 + C, sl]
                qc = qs[r0:r0 + C, sl]
                kc = kd[r0:r0 + C, sl]
                vc = v[r0:r0 + C, sl]
                b_last = bc[C - 1:C, :]
                o_inter = _dot_nt(_bf(qc * jnp.exp(bc)), _bf(st))
                o_blocks = []
                for i in range(nb):
                    rs = slice(i * R, (i + 1) * R)
                    bq = bc[rs]
                    qq = qc[rs]
                    acc = jnp.zeros((R, HG_DK), F32)
                    if i > 0:
                        anchor = bc[i * R - 1:i * R, :]
                        early = cidx < i * R
                        qa = qq * jnp.exp(bq - anchor)
                        ka = jnp.where(early, kc * jnp.exp(jnp.where(early, anchor - bc, 0.0)), 0.0)
                        att = _dot_nt(_bf(qa), _bf(ka))
                        acc = acc + _dot(_bf(att), _bf(vc))
                    for s in range(R):
                        sa = i * R + s
                        msk = tidx >= s
                        e = jnp.exp(jnp.where(msk, bq - bc[sa:sa + 1, :], 0.0))
                        term = jnp.where(msk, qq * kc[sa:sa + 1, :] * e, 0.0)
                        acc = acc + jnp.sum(term, axis=-1, keepdims=True) * vc[sa:sa + 1, :]
                    o_blocks.append(acc)
                o_intra = o_blocks[0] if nb == 1 else jnp.concatenate(o_blocks, axis=0)
                o_chunks.append(o_inter + o_intra)
                k_out = kc * jnp.exp(b_last - bc)
                st = st * jnp.exp(b_last) + _dot(_bf(vc.T), _bf(k_out))
            st_ref[b, h] = st
            o_h = o_chunks[0] if len(o_chunks) == 1 else jnp.concatenate(o_chunks, axis=0)
            o_h = o_h * lax.rsqrt(jnp.mean(o_h * o_h, axis=-1, keepdims=True) + HEAD_NORM_EPS) * ng[:, sl]
            o_heads.append(o_h)
        out_ref[b] = jnp.concatenate(o_heads, axis=-1) * _silu(p_ref[b, :, 1536:2048])

    @pl.when(ti == nt - 1)
    def _():
        for b in range(Bb):
            for h in range(HG_HEADS):
                sfin_ref[b, h] = st_ref[b, h].T


def _hgrn(p_hg, s0, lb_raw, norm_g, *, Bb, Tb, C, R):
    B, T, _ = p_hg.shape
    kern = functools.partial(_hgrn_kernel, Bb=Bb, Tb=Tb, C=C, R=R)
    return pl.pallas_call(
        kern,
        out_shape=(jax.ShapeDtypeStruct((B, T, HG_WIDTH), F32),
                   jax.ShapeDtypeStruct((B, HG_HEADS, HG_DK, HG_DK), F32)),
        grid=(B // Bb, T // Tb),
        in_specs=[pl.BlockSpec((Bb, Tb, HG_COLS), lambda i, j: (i, j, 0)),
                  pl.BlockSpec((Bb, HG_HEADS, HG_DK, HG_DK), lambda i, j: (i, 0, 0, 0)),
                  pl.BlockSpec((DEPTH + 1, HG_WIDTH), lambda i, j: (0, 0)),
                  pl.BlockSpec((1, HG_WIDTH), lambda i, j: (0, 0))],
        out_specs=(pl.BlockSpec((Bb, Tb, HG_WIDTH), lambda i, j: (i, j, 0)),
                   pl.BlockSpec((Bb, HG_HEADS, HG_DK, HG_DK), lambda i, j: (i, 0, 0, 0))),
        scratch_shapes=[pltpu.VMEM((Bb, HG_HEADS, HG_DK, HG_DK), F32)],
        compiler_params=_params(("parallel", "arbitrary")),
        name="hgrn2_mixer",
    )(p_hg, s0, lb_raw, norm_g)


RW_PAIRS = RW_HEADS // 2
RW_SEQ = 8


def _block_ones():
    i = lax.broadcasted_iota(jnp.int32, (LANES, LANES), 0)
    j = lax.broadcasted_iota(jnp.int32, (LANES, LANES), 1)
    return _bf(jnp.where((i // RW_DH) == (j // RW_DH), 1.0, 0.0))


def _seg_sum(x, bd):
    outs = []
    for c in range(RW_WIDTH // LANES):
        hi, lo = _split2(x[:, c * LANES:(c + 1) * LANES])
        outs.append(_dot(hi, bd) + _dot(lo, bd))
    return jnp.concatenate(outs, axis=-1)


def _rwkv_kernel(p_ref, shift_ref, s0_ref, mu_ref, w0_ref, wa_ref, a0_ref, g2_ref, kk_ref, ka_ref,
                 rk_ref, gng_ref, gnb_ref, out_ref, sfin_ref,
                 s_ref, prev_ref, r_s, w_s, k_s, v_s, kk_s, kka_s, g_s, vt_s, o_s, *, tb):
    ti = pl.program_id(1)
    nt = pl.num_programs(1)
    pairs = range(RW_PAIRS)
    sub = 8

    @pl.when(ti == 0)
    def _():
        prev_ref[...] = shift_ref[...]
        for b in range(RW_SEQ):
            for p in pairs:
                s_ref[b, p * RW_DH:(p + 1) * RW_DH, :] = jnp.concatenate(
                    [s0_ref[b, 2 * p], s0_ref[b, 2 * p + 1]], axis=-1)

    bd = _block_ones()
    lane = lax.broadcasted_iota(jnp.int32, (1, LANES), 1)
    rid = lax.broadcasted_iota(jnp.int32, (tb, 1), 0)
    mu = mu_ref[...]

    for b in range(RW_SEQ):
        p = p_ref[b]
        prev = jnp.where(rid == 0, prev_ref[b:b + 1, :], pltpu.roll(p, 1, axis=0))
        prev_ref[b:b + 1, :] = p[tb - 1:tb, :]
        ps = p + mu * (prev - p)
        r = ps[:, 0:512]
        k = ps[:, 512:1024]
        v = ps[:, 1024:1536]
        zz = ps[:, 1536:1664]
        zg = ps[:, 1664:1792]
        lora = _dot(_bf(jnp.where(lane < 64, jnp.tanh(zz), zz)), wa_ref[...])
        wl = -(w0_ref[...] + lora[:, 0:512])
        w_raw = -(jnp.maximum(wl, 0.0) + jnp.log(1.0 + jnp.exp(-jnp.abs(wl)))) - 0.5
        a = _sigmoid(a0_ref[...] + lora[:, 512:1024])
        kk = k * kk_ref[...]
        kk = kk / jnp.maximum(jnp.sqrt(_seg_sum(kk * kk, bd)), 1e-12)
        r_s[b] = r
        w_s[b] = jnp.exp(-jnp.exp(w_raw))
        k_s[b] = k * (1.0 + (a - 1.0) * ka_ref[...])
        v_s[b] = v
        kk_s[b] = kk
        kka_s[b] = kk * a
        g_s[b] = _dot(_bf(_sigmoid(zg)), g2_ref[...])
        for q in pairs:
            vp = v[:, q * LANES:(q + 1) * LANES]
            if tb < RW_DH:
                vp = jnp.concatenate([vp, jnp.zeros((RW_DH - tb, LANES), F32)], axis=0)
            vpt = vp.T
            vt_s[b, q * RW_DH:(q + 1) * RW_DH, :] = _bf(
                jnp.concatenate([vpt[0:RW_DH, :], vpt[RW_DH:LANES, :]], axis=1))

    ci = lax.broadcasted_iota(jnp.int32, (LANES, LANES), 0)
    cj = lax.broadcasted_iota(jnp.int32, (LANES, LANES), 1)
    same_head = (ci // RW_DH) == (cj // RW_DH)
    si = lax.broadcasted_iota(jnp.int32, (16, 2 * LANES), 0)
    sj = lax.broadcasted_iota(jnp.int32, (16, 2 * LANES), 1)
    head_rows = _bf(jnp.where(si == sj // RW_DH, 1.0, 0.0))

    def rows(ref, b, t0, j):
        return [ref[b, pl.ds(t0, sub), pl.ds(q * LANES, LANES)][j:j + 1, :] for q in pairs]

    def scale(x, rws):
        return jnp.concatenate([x[q * RW_DH:(q + 1) * RW_DH] * rws[q] for q in pairs], axis=0)

    def side_by_side(x):
        return jnp.concatenate(
            [jnp.concatenate([x[0:RW_DH], x[RW_DH:2 * RW_DH]], axis=1),
             jnp.concatenate([x[2 * RW_DH:3 * RW_DH], x[3 * RW_DH:4 * RW_DH]], axis=1)], axis=0)

    def steps(tq, carry):
        t0 = pl.multiple_of(tq * sub, sub)
        for j in range(sub):
            tl = tq * sub + j
            pick = _bf(jnp.where(same_head & ((ci % RW_DH) == tl), 1.0, 0.0))
            sa = [_dot(_bf(scale(s_ref[b], rows(kk_s, b, t0, j))), bd) for b in range(RW_SEQ)]
            vb = [_dot(vt_s[b], pick) for b in range(RW_SEQ)]
            for b in range(RW_SEQ):
                s = (scale(s_ref[b], rows(w_s, b, t0, j)) - scale(sa[b], rows(kka_s, b, t0, j))
                     + scale(vb[b], rows(k_s, b, t0, j)))
                s_ref[b] = s
                o4 = _dot_nt(head_rows, _bf(side_by_side(scale(s, rows(r_s, b, t0, j)))))
                for h in range(RW_PAIRS):
                    o_s[h, tq, j:j + 1, b * LANES:(b + 1) * LANES] = o4[h:h + 1, :]
        return carry

    lax.fori_loop(0, tb // sub, steps, 0)

    for b in range(RW_SEQ):
        lanes = slice(b * LANES, (b + 1) * LANES)
        nq = tb // sub
        oh = [jnp.concatenate([o_s[h, q, :, lanes] for q in range(nq)], axis=0) if nq > 1
              else o_s[h, 0, :, lanes] for h in range(RW_PAIRS)]
        o = jnp.concatenate([oh[h % RW_PAIRS][:, (h // RW_PAIRS) * RW_DH:(h // RW_PAIRS + 1) * RW_DH]
                             for h in range(RW_HEADS)], axis=-1)
        d = o - _seg_sum(o, bd) * (1.0 / RW_DH)
        var = _seg_sum(d * d, bd) * (1.0 / RW_DH)
        on = d * lax.rsqrt(var + RW_GN_EPS) * gng_ref[...] + gnb_ref[...]
        bonus = _seg_sum(r_s[b] * k_s[b] * rk_ref[...], bd) * v_s[b]
        out_ref[b] = (on + bonus) * g_s[b]

    @pl.when(ti == nt - 1)
    def _():
        for b in range(RW_SEQ):
            for p in pairs:
                s = s_ref[b, p * RW_DH:(p + 1) * RW_DH, :]
                sfin_ref[b, 2 * p] = s[:, 0:RW_DH]
                sfin_ref[b, 2 * p + 1] = s[:, RW_DH:LANES]


def _rwkv(p_rw, shift, s0, mu, w0, wa, a0, g2, k_k, k_a, r_k, gn_g, gn_b, *, tb):
    B, T, _ = p_rw.shape
    kern = functools.partial(_rwkv_kernel, tb=tb)
    vec = lambda n: pl.BlockSpec((1, n), lambda i, j: (0, 0))
    blk = pltpu.VMEM((RW_SEQ, tb, RW_WIDTH), F32)
    oblk = pltpu.VMEM((RW_PAIRS, tb // 8, 8, RW_SEQ * LANES), F32)
    return pl.pallas_call(
        kern,
        out_shape=(jax.ShapeDtypeStruct((B, T, RW_WIDTH), F32),
                   jax.ShapeDtypeStruct((B, RW_HEADS, RW_DH, RW_DH), F32)),
        grid=(B // RW_SEQ, T // tb),
        in_specs=[pl.BlockSpec((RW_SEQ, tb, RW_COLS), lambda i, j: (i, j, 0)),
                  pl.BlockSpec((RW_SEQ, RW_COLS), lambda i, j: (i, 0)),
                  pl.BlockSpec((RW_SEQ, RW_HEADS, RW_DH, RW_DH), lambda i, j: (i, 0, 0, 0)),
                  vec(RW_COLS), vec(RW_WIDTH),
                  pl.BlockSpec((LANES, 2 * RW_WIDTH), lambda i, j: (0, 0)),
                  vec(RW_WIDTH),
                  pl.BlockSpec((LANES, RW_WIDTH), lambda i, j: (0, 0)),
                  vec(RW_WIDTH), vec(RW_WIDTH), vec(RW_WIDTH), vec(RW_WIDTH), vec(RW_WIDTH)],
        out_specs=(pl.BlockSpec((RW_SEQ, tb, RW_WIDTH), lambda i, j: (i, j, 0)),
                   pl.BlockSpec((RW_SEQ, RW_HEADS, RW_DH, RW_DH), lambda i, j: (i, 0, 0, 0))),
        scratch_shapes=[pltpu.VMEM((RW_SEQ, RW_PAIRS * RW_DH, LANES), F32),
                        pltpu.VMEM((RW_SEQ, RW_COLS), F32),
                        blk, blk, blk, blk, blk, blk, blk,
                        pltpu.VMEM((RW_SEQ, RW_PAIRS * RW_DH, LANES), BF16),
                        oblk],
        compiler_params=_params(("parallel", "arbitrary")),
        name="rwkv7_mixer",
    )(p_rw, shift, s0, mu, w0, wa, a0, g2, k_k, k_a, r_k, gn_g, gn_b)


def _layer_norm(y, g, b):
    mu = jnp.mean(y, axis=-1, keepdims=True)
    d = y - mu
    var = jnp.mean(d * d, axis=-1, keepdims=True)
    return d * lax.rsqrt(var + LN_EPS) * g + b


def _outproj_kernel(x_ref, hg_ref, rw_ref, w_ref, g_ref, b_ref, h_ref):
    mix = (_dot(_bf(hg_ref[...]), w_ref[0:HG_WIDTH, :])
           + _dot(_bf(rw_ref[...]), w_ref[HG_WIDTH:HG_WIDTH + RW_WIDTH, :]))
    h_ref[...] = _layer_norm(ALPHA * x_ref[...] + mix, g_ref[...], b_ref[...])


def _outproj(x2d, hg2d, rw2d, w_out_bf, ln_g, ln_b, tm):
    n = x2d.shape[0]
    rows = lambda w: pl.BlockSpec((tm, w), lambda i: (i, 0))
    return pl.pallas_call(
        _outproj_kernel,
        out_shape=jax.ShapeDtypeStruct((n, D_MODEL), F32),
        grid=(n // tm,),
        in_specs=[rows(D_MODEL), rows(HG_WIDTH), rows(RW_WIDTH),
                  pl.BlockSpec((D_MODEL, D_MODEL), lambda i: (0, 0)),
                  pl.BlockSpec((1, D_MODEL), lambda i: (0, 0)),
                  pl.BlockSpec((1, D_MODEL), lambda i: (0, 0))],
        out_specs=rows(D_MODEL),
        compiler_params=_params(("parallel",)),
        name="out_proj_ln",
    )(x2d, hg2d, rw2d, w_out_bf, ln_g, ln_b)


def _router_kernel(h_ref, w_ref, eb_ref, gates_ref):
    tm = h_ref.shape[0]
    hh, hl = _split2(h_ref[...])
    wh, wl = _split2(w_ref[...])
    logits = _dot(hh, wh) + (_dot(hh, wl) + _dot(hl, wh))
    lane = lax.broadcasted_iota(jnp.int32, (tm, LANES), 1)
    valid = lane < N_EXPERTS
    neg = -jnp.inf
    scores = _sigmoid(logits)
    choice = jnp.where(valid, scores + eb_ref[...], neg)

    def partner(x, s):
        return jnp.where((lane & s) != 0, pltpu.roll(x, s, axis=1), pltpu.roll(x, LANES - s, axis=1))

    def group_max(x):
        for s in (1, 2, 4):
            x = jnp.maximum(x, partner(x, s))
        return x

    def group_min(x):
        for s in (1, 2, 4):
            x = jnp.minimum(x, partner(x, s))
        return x

    m1 = group_max(choice)
    first = group_min(jnp.where(choice == m1, lane, LANES))
    m2 = group_max(jnp.where(lane == first, neg, choice))
    gscore = m1 + m2
    gper = jnp.where(valid, gscore, pltpu.roll(gscore, N_EXPERTS, axis=1))
    gidx = (lane % N_EXPERTS) // GROUP_SIZE
    rank = jnp.zeros((tm, LANES), jnp.int32)
    for j in range(1, N_GROUPS):
        other = pltpu.roll(gper, GROUP_SIZE * j, axis=1)
        beats = (other > gper) | ((other == gper) & (gidx >= j))
        rank = rank + beats.astype(jnp.int32)
    cand = jnp.where(valid & (rank < TOPK_GROUPS), choice, neg)
    sel = jnp.zeros((tm, LANES), jnp.bool_)
    for _ in range(TOP_K):
        m = jnp.max(cand, axis=-1, keepdims=True)
        idx = jnp.min(jnp.where(cand == m, lane, LANES), axis=-1, keepdims=True)
        hit = lane == idx
        sel = sel | hit
        cand = jnp.where(hit, neg, cand)
    wts = jnp.where(sel, scores, 0.0)
    gates = wts / jnp.sum(wts, axis=-1, keepdims=True) * ROUTED_SCALE
    gates_ref[...] = gates[:, 0:N_EXPERTS]


def _router(h2d, w_router_pad, e_bias_pad, tm):
    n = h2d.shape[0]
    return pl.pallas_call(
        _router_kernel,
        out_shape=jax.ShapeDtypeStruct((n, N_EXPERTS), F32),
        grid=(n // tm,),
        in_specs=[pl.BlockSpec((tm, D_MODEL), lambda i: (i, 0)),
                  pl.BlockSpec((D_MODEL, LANES), lambda i: (0, 0)),
                  pl.BlockSpec((1, LANES), lambda i: (0, 0))],
        out_specs=pl.BlockSpec((tm, N_EXPERTS), lambda i: (i, 0)),
        compiler_params=_params(("parallel",)),
        name="router",
    )(h2d, w_router_pad, e_bias_pad)


def _moe_kernel(h_ref, gate_ref, wg_ref, wu_ref, wd_ref, sg_ref, su_ref, sd_ref, g_ref, b_ref,
                out_ref, acc_ref, xb_ref):
    e = pl.program_id(1)

    @pl.when(e == 0)
    def _():
        xb = _bf(h_ref[...])
        xb_ref[...] = xb
        act = _silu(_dot(xb, sg_ref[...])) * _dot(xb, su_ref[...])
        acc_ref[...] = _dot(_bf(act), sd_ref[...])

    xb = xb_ref[...]
    hg = _dot(xb, _bf(wg_ref[0]))
    hu = _dot(xb, _bf(wu_ref[0]))
    act = _silu(hg) * hu * gate_ref[0]
    acc_ref[...] += _dot(_bf(act), _bf(wd_ref[0]))

    @pl.when(e == pl.num_programs(1) - 1)
    def _():
        out_ref[...] = _layer_norm(ALPHA * h_ref[...] + acc_ref[...], g_ref[...], b_ref[...])


def _moe(h2d, gates_t, w_gate, w_up, w_down, sg_bf, su_bf, sd_bf, ln_g, ln_b, tm):
    n = h2d.shape[0]
    const = lambda shape: pl.BlockSpec(shape, lambda i, e: (0,) * len(shape))
    return pl.pallas_call(
        _moe_kernel,
        out_shape=jax.ShapeDtypeStruct((n, D_MODEL), F32),
        grid=(n // tm, N_EXPERTS),
        in_specs=[pl.BlockSpec((tm, D_MODEL), lambda i, e: (i, 0)),
                  pl.BlockSpec((1, tm, 1), lambda i, e: (e, i, 0)),
                  pl.BlockSpec((1, D_MODEL, D_EXPERT), lambda i, e: (e, 0, 0)),
                  pl.BlockSpec((1, D_MODEL, D_EXPERT), lambda i, e: (e, 0, 0)),
                  pl.BlockSpec((1, D_EXPERT, D_MODEL), lambda i, e: (e, 0, 0)),
                  const((D_MODEL, D_EXPERT)), const((D_MODEL, D_EXPERT)), const((D_EXPERT, D_MODEL)),
                  const((1, D_MODEL)), const((1, D_MODEL))],
        out_specs=pl.BlockSpec((tm, D_MODEL), lambda i, e: (i, 0)),
        scratch_shapes=[pltpu.VMEM((tm, D_MODEL), F32), pltpu.VMEM((tm, D_MODEL), BF16)],
        compiler_params=_params(("parallel", "arbitrary")),
        name="moe_experts",
    )(h2d, gates_t, w_gate, w_up, w_down, sg_bf, su_bf, sd_bf, ln_g, ln_b)


def _row_tile(n, cap):
    t = cap
    while n % t:
        t //= 2
    return t


def _trunk(x, s_hg, s_rw, shift, w, *, hg_cfg, rw_tb):
    B, T, _ = x.shape
    n = B * T
    x2d = x.reshape(n, D_MODEL)
    p_hg, p_rw = _proj(x2d, w["w_in"], _row_tile(n, 512))
    p_hg = p_hg.reshape(B, T, HG_COLS)
    p_rw = p_rw.reshape(B, T, RW_COLS)
    hg_out, hg_state = _hgrn(p_hg, s_hg, w["hg_lb"], w["hg_norm_g"], **hg_cfg)
    rw_out, rw_state = _rwkv(p_rw, shift, s_rw, w["rw_mu"], w["rw_w0"], w["rw_wa"], w["rw_a0"],
                             w["rw_g2"], w["rw_k_k"], w["rw_k_a"], w["rw_r_k"], w["rw_gn_g"],
                             w["rw_gn_b"], tb=rw_tb)
    h = _outproj(x2d, hg_out.reshape(n, HG_WIDTH), rw_out.reshape(n, RW_WIDTH), w["w_out"],
                 w["ln1_g"], w["ln1_b"], _row_tile(n, 512))
    gates = _router(h, w["w_router"], w["e_bias"], _row_tile(n, 512))
    gates_t = gates.T.reshape(N_EXPERTS, n, 1)
    y = _moe(h, gates_t, w["w_gate"], w["w_up"], w["w_down"], w["ws_gate"], w["ws_up"], w["ws_down"],
             w["ln2_g"], w["ln2_b"], _row_tile(n, 1024))
    return y.reshape(B, T, D_MODEL), hg_state, rw_state, p_rw[:, T - 1, :]


def kernel(x_prompt, x_sample, state_hgrn, state_rwkv, state_shift, w_in, w_out, hg_lb, hg_norm_g, rw_mu, rw_w0, rw_w2, rw_a0, rw_a2, rw_g2, rw_k_k, rw_k_a, rw_r_k, rw_gn_g, rw_gn_b, ln1_g, ln1_b, ln2_g, ln2_b, w_router, e_bias, w_gate, w_up, w_down, ws_gate, ws_up, ws_down):
    l = 0
    zeros_lora = jnp.zeros((RW_DH, RW_WIDTH), F32)
    w = {
        "w_in": _bf(w_in[l]), "w_out": _bf(w_out[l]),
        "hg_lb": hg_lb, "hg_norm_g": hg_norm_g[l][None, :],
        "rw_mu": rw_mu[l][None, :], "rw_w0": rw_w0[l][None, :], "rw_a0": rw_a0[l][None, :],
        "rw_wa": _bf(jnp.concatenate([jnp.concatenate([rw_w2[l], zeros_lora], axis=1),
                                      jnp.concatenate([zeros_lora, rw_a2[l]], axis=1)], axis=0)),
        "rw_g2": _bf(rw_g2[l]),
        "rw_k_k": rw_k_k[l][None, :], "rw_k_a": rw_k_a[l][None, :],
        "rw_r_k": rw_r_k[l].reshape(1, RW_WIDTH),
        "rw_gn_g": rw_gn_g[l][None, :], "rw_gn_b": rw_gn_b[l][None, :],
        "ln1_g": ln1_g[l][None, :], "ln1_b": ln1_b[l][None, :],
        "ln2_g": ln2_g[l][None, :], "ln2_b": ln2_b[l][None, :],
        "w_router": jnp.pad(w_router[l], ((0, 0), (0, LANES - N_EXPERTS))),
        "e_bias": jnp.pad(e_bias[l], (0, LANES - N_EXPERTS))[None, :],
        "w_gate": w_gate[l], "w_up": w_up[l], "w_down": w_down[l],
        "ws_gate": _bf(ws_gate[l]), "ws_up": _bf(ws_up[l]), "ws_down": _bf(ws_down[l]),
    }
    Bp = x_prompt.shape[0]
    zero_hg = jnp.zeros((Bp, HG_HEADS, HG_DK, HG_DK), F32)
    zero_rw = jnp.zeros((Bp, RW_HEADS, RW_DH, RW_DH), F32)
    zero_sh = jnp.zeros((Bp, RW_COLS), F32)
    yp, hg_p, rw_p, sh_p = _trunk(x_prompt, zero_hg, zero_rw, zero_sh, w,
                                  hg_cfg=dict(Bb=1, Tb=128, C=64, R=16), rw_tb=64)
    ys, hg_s, rw_s, sh_s = _trunk(x_sample, state_hgrn[l], state_rwkv[l], state_shift[l], w,
                                  hg_cfg=dict(Bb=8, Tb=8, C=8, R=8), rw_tb=8)
    return (yp, ys, hg_p[None], rw_p[None], sh_p[None], hg_s[None], rw_s[None], sh_s[None])
```

```python
import functools

import jax
import jax.numpy as jnp
from jax import lax
from jax.experimental import pallas as pl
from jax.experimental.pallas import tpu as pltpu

F32 = jnp.float32
BF16 = jnp.bfloat16

D_MODEL = 1024
HG_HEADS = 4
HG_DK = 128
HG_WIDTH = 512
HG_COLS = 2048
RW_HEADS = 8
RW_DH = 64
RW_WIDTH = 512
RW_COLS = 1792
N_EXPERTS = 64
TOP_K = 6
N_GROUPS = 8
TOPK_GROUPS = 4
GROUP_SIZE = N_EXPERTS // N_GROUPS
D_EXPERT = 256
ROUTED_SCALE = 2.5
DEPTH = 1
ALPHA = (2.0 * DEPTH) ** 0.25
LN_EPS = 1e-5
HEAD_NORM_EPS = 1e-6
RW_GN_EPS = 64e-5

LANES = 128
VMEM_LIMIT = 56 * 1024 * 1024


def _bf(x):
    return x.astype(BF16)


def _dot(a, b):
    return jnp.dot(a, b, preferred_element_type=F32)


def _dot_nt(a, b):
    return lax.dot_general(a, b, (((1,), (1,)), ((), ())), preferred_element_type=F32)


def _split2(x):
    hi = _bf(x)
    lo = _bf(x - hi.astype(F32))
    return hi, lo


def _split3(x):
    hi = _bf(x)
    r1 = x - hi.astype(F32)
    mid = _bf(r1)
    lo = _bf(r1 - mid.astype(F32))
    return hi, mid, lo


def _sigmoid(x):
    return 1.0 / (1.0 + jnp.exp(-x))


def _silu(x):
    return x * _sigmoid(x)


def _params(sem):
    return pltpu.CompilerParams(dimension_semantics=sem, vmem_limit_bytes=VMEM_LIMIT)


def _proj_kernel(x_ref, w_ref, hg_ref, rw_ref):
    xb = _bf(x_ref[...])
    step = 256
    for c0 in range(0, HG_COLS, step):
        hg_ref[:, c0:c0 + step] = _dot(xb, w_ref[:, c0:c0 + step])
    for c0 in range(0, RW_COLS, step):
        rw_ref[:, c0:c0 + step] = _dot(xb, w_ref[:, HG_COLS + c0:HG_COLS + c0 + step])


def _proj(x2d, w_in_bf, tm):
    n = x2d.shape[0]
    return pl.pallas_call(
        _proj_kernel,
        out_shape=(jax.ShapeDtypeStruct((n, HG_COLS), F32), jax.ShapeDtypeStruct((n, RW_COLS), F32)),
        grid=(n // tm,),
        in_specs=[pl.BlockSpec((tm, D_MODEL), lambda i: (i, 0)),
                  pl.BlockSpec((D_MODEL, HG_COLS + RW_COLS), lambda i: (0, 0))],
        out_specs=(pl.BlockSpec((tm, HG_COLS), lambda i: (i, 0)),
                   pl.BlockSpec((tm, RW_COLS), lambda i: (i, 0))),
        compiler_params=_params(("parallel",)),
        name="in_proj",
    )(x2d, w_in_bf)


def _hgrn_kernel(p_ref, s0_ref, lbraw_ref, ng_ref, out_ref, sfin_ref, st_ref, *, Bb, Tb, C, R):
    ti = pl.program_id(1)
    nt = pl.num_programs(1)
    nb = C // R

    @pl.when(ti == 0)
    def _():
        for b in range(Bb):
            for h in range(HG_HEADS):
                st_ref[b, h] = s0_ref[b, h].T

    lbr = lbraw_ref[...]
    lbe = jnp.exp(lbr - jnp.max(lbr, axis=0, keepdims=True))
    lb = lbe[0:1, :] / jnp.sum(lbe, axis=0, keepdims=True)
    ng = ng_ref[...]

    if Tb > 8:
        rows = lax.broadcasted_iota(jnp.int32, (Tb, Tb), 0)
        cols = lax.broadcasted_iota(jnp.int32, (Tb, Tb), 1)
        tril = _bf(jnp.where(((rows // C) == (cols // C)) & (cols <= rows), 1.0, 0.0))
    tidx = lax.broadcasted_iota(jnp.int32, (R, 1), 0)
    cidx = lax.broadcasted_iota(jnp.int32, (C, 1), 0)

    for b in range(Bb):
        qs = _silu(p_ref[b, :, 0:512])
        f = lb + (1.0 - lb) * _sigmoid(p_ref[b, :, 512:1024])
        lf = jnp.log(f)
        kd = 1.0 - f
        v = p_ref[b, :, 1024:1536]
        if Tb == 8:
            r8 = lax.broadcasted_iota(jnp.int32, (8, 1), 0)
            bcum = jnp.concatenate(
                [jnp.sum(jnp.where(r8 <= t, lf, 0.0), axis=0, keepdims=True) for t in range(8)], axis=0)
        else:
            hi, mid, lo = _split3(lf)
            bcum = _dot(tril, hi) + _dot(tril, mid) + _dot(tril, lo)
        o_heads = []
        for h in range(HG_HEADS):
            sl = slice(h * HG_DK, (h + 1) * HG_DK)
            st = st_ref[b, h]
            o_chunks = []
            for c in range(Tb // C):
                r0 = c * C
                bc = bcum[r0:r0 + C, sl]
                qc = qs[r0:r0 + C, sl]
                kc = kd[r0:r0 + C, sl]
                vc = v[r0:r0 + C, sl]
                b_last = bc[C - 1:C, :]
                o_inter = _dot_nt(_bf(qc * jnp.exp(bc)), _bf(st))
                o_blocks = []
                for i in range(nb):
                    rs = slice(i * R, (i + 1) * R)
                    bq = bc[rs]
                    qq = qc[rs]
                    acc = jnp.zeros((R, HG_DK), F32)
                    if i > 0:
                        anchor = bc[i * R - 1:i * R, :]
                        early = cidx < i * R
                        qa = qq * jnp.exp(bq - anchor)
                        ka = jnp.where(early, kc * jnp.exp(jnp.where(early, anchor - bc, 0.0)), 0.0)
                        att = _dot_nt(_bf(qa), _bf(ka))
                        acc = acc + _dot(_bf(att), _bf(vc))
                    for s in range(R):
                        sa = i * R + s
                        msk = tidx >= s
                        e = jnp.exp(jnp.where(msk, bq - bc[sa:sa + 1, :], 0.0))
                        term = jnp.where(msk, qq * kc[sa:sa + 1, :] * e, 0.0)
                        acc = acc + jnp.sum(term, axis=-1, keepdims=True) * vc[sa:sa + 1, :]
                    o_blocks.append(acc)
                o_intra = o_blocks[0] if nb == 1 else jnp.concatenate(o_blocks, axis=0)
                o_chunks.append(o_inter + o_intra)
                k_out = kc * jnp.exp(b_last - bc)
                st = st * jnp.exp(b_last) + _dot(_bf(vc.T), _bf(k_out))
            st_ref[b, h] = st
            o_h = o_chunks[0] if len(o_chunks) == 1 else jnp.concatenate(o_chunks, axis=0)
            o_h = o_h * lax.rsqrt(jnp.mean(o_h * o_h, axis=-1, keepdims=True) + HEAD_NORM_EPS) * ng[:, sl]
            o_heads.append(o_h)
        out_ref[b] = jnp.concatenate(o_heads, axis=-1) * _silu(p_ref[b, :, 1536:2048])

    @pl.when(ti == nt - 1)
    def _():
        for b in range(Bb):
            for h in range(HG_HEADS):
                sfin_ref[b, h] = st_ref[b, h].T


def _hgrn(p_hg, s0, lb_raw, norm_g, *, Bb, Tb, C, R):
    B, T, _ = p_hg.shape
    kern = functools.partial(_hgrn_kernel, Bb=Bb, Tb=Tb, C=C, R=R)
    return pl.pallas_call(
        kern,
        out_shape=(jax.ShapeDtypeStruct((B, T, HG_WIDTH), F32),
                   jax.ShapeDtypeStruct((B, HG_HEADS, HG_DK, HG_DK), F32)),
        grid=(B // Bb, T // Tb),
        in_specs=[pl.BlockSpec((Bb, Tb, HG_COLS), lambda i, j: (i, j, 0)),
                  pl.BlockSpec((Bb, HG_HEADS, HG_DK, HG_DK), lambda i, j: (i, 0, 0, 0)),
                  pl.BlockSpec((DEPTH + 1, HG_WIDTH), lambda i, j: (0, 0)),
                  pl.BlockSpec((1, HG_WIDTH), lambda i, j: (0, 0))],
        out_specs=(pl.BlockSpec((Bb, Tb, HG_WIDTH), lambda i, j: (i, j, 0)),
                   pl.BlockSpec((Bb, HG_HEADS, HG_DK, HG_DK), lambda i, j: (i, 0, 0, 0))),
        scratch_shapes=[pltpu.VMEM((Bb, HG_HEADS, HG_DK, HG_DK), F32)],
        compiler_params=_params(("parallel", "arbitrary")),
        name="hgrn2_mixer",
    )(p_hg, s0, lb_raw, norm_g)


RW_PAIRS = RW_HEADS // 2
RW_SEQ = 8


def _block_ones():
    i = lax.broadcasted_iota(jnp.int32, (LANES, LANES), 0)
    j = lax.broadcasted_iota(jnp.int32, (LANES, LANES), 1)
    return _bf(jnp.where((i // RW_DH) == (j // RW_DH), 1.0, 0.0))


def _seg_sum(x, bd):
    outs = []
    for c in range(RW_WIDTH // LANES):
        hi, lo = _split2(x[:, c * LANES:(c + 1) * LANES])
        outs.append(_dot(hi, bd) + _dot(lo, bd))
    return jnp.concatenate(outs, axis=-1)


def _rwkv_kernel(p_ref, shift_ref, s0_ref, mu_ref, w0_ref, wa_ref, a0_ref, g2_ref, kk_ref, ka_ref,
                 rk_ref, gng_ref, gnb_ref, out_ref, sfin_ref,
                 s_ref, prev_ref, r_s, w_s, k_s, v_s, kk_s, kka_s, g_s, vt_s, o_s, *, tb):
    ti = pl.program_id(1)
    nt = pl.num_programs(1)
    pairs = range(RW_PAIRS)
    sub = 8

    @pl.when(ti == 0)
    def _():
        prev_ref[...] = shift_ref[...]
        for b in range(RW_SEQ):
            for p in pairs:
                s_ref[b, p * RW_DH:(p + 1) * RW_DH, :] = jnp.concatenate(
                    [s0_ref[b, 2 * p], s0_ref[b, 2 * p + 1]], axis=-1)

    bd = _block_ones()
    lane = lax.broadcasted_iota(jnp.int32, (1, LANES), 1)
    rid = lax.broadcasted_iota(jnp.int32, (tb, 1), 0)
    mu = mu_ref[...]

    for b in range(RW_SEQ):
        p = p_ref[b]
        prev = jnp.where(rid == 0, prev_ref[b:b + 1, :], pltpu.roll(p, 1, axis=0))
        prev_ref[b:b + 1, :] = p[tb - 1:tb, :]
        ps = p + mu * (prev - p)
        r = ps[:, 0:512]
        k = ps[:, 512:1024]
        v = ps[:, 1024:1536]
        zz = ps[:, 1536:1664]
        zg = ps[:, 1664:1792]
        lora = _dot(_bf(jnp.where(lane < 64, jnp.tanh(zz), zz)), wa_ref[...])
        wl = -(w0_ref[...] + lora[:, 0:512])
        w_raw = -(jnp.maximum(wl, 0.0) + jnp.log(1.0 + jnp.exp(-jnp.abs(wl)))) - 0.5
        a = _sigmoid(a0_ref[...] + lora[:, 512:1024])
        kk = k * kk_ref[...]
        kk = kk / jnp.maximum(jnp.sqrt(_seg_sum(kk * kk, bd)), 1e-12)
        r_s[b] = r
        w_s[b] = jnp.exp(-jnp.exp(w_raw))
        k_s[b] = k * (1.0 + (a - 1.0) * ka_ref[...])
        v_s[b] = v
        kk_s[b] = kk
        kka_s[b] = kk * a
        g_s[b] = _dot(_bf(_sigmoid(zg)), g2_ref[...])
        for q in pairs:
            vp = v[:, q * LANES:(q + 1) * LANES]
            if tb < RW_DH:
                vp = jnp.concatenate([vp, jnp.zeros((RW_DH - tb, LANES), F32)], axis=0)
            vpt = vp.T
            vt_s[b, q * RW_DH:(q + 1) * RW_DH, :] = _bf(
                jnp.concatenate([vpt[0:RW_DH, :], vpt[RW_DH:LANES, :]], axis=1))

    ci = lax.broadcasted_iota(jnp.int32, (LANES, LANES), 0)
    cj = lax.broadcasted_iota(jnp.int32, (LANES, LANES), 1)
    same_head = (ci // RW_DH) == (cj // RW_DH)
    si = lax.broadcasted_iota(jnp.int32, (16, 2 * LANES), 0)
    sj = lax.broadcasted_iota(jnp.int32, (16, 2 * LANES), 1)
    head_rows = _bf(jnp.where(si == sj // RW_DH, 1.0, 0.0))

    def rows(ref, b, t0, j):
        return [ref[b, pl.ds(t0, sub), pl.ds(q * LANES, LANES)][j:j + 1, :] for q in pairs]

    def scale(x, rws):
        return jnp.concatenate([x[q * RW_DH:(q + 1) * RW_DH] * rws[q] for q in pairs], axis=0)

    def side_by_side(x):
        return jnp.concatenate(
            [jnp.concatenate([x[0:RW_DH], x[RW_DH:2 * RW_DH]], axis=1),
             jnp.concatenate([x[2 * RW_DH:3 * RW_DH], x[3 * RW_DH:4 * RW_DH]], axis=1)], axis=0)

    def steps(tq, carry):
        t0 = pl.multiple_of(tq * sub, sub)
        for j in range(sub):
            tl = tq * sub + j
            pick = _bf(jnp.where(same_head & ((ci % RW_DH) == tl), 1.0, 0.0))
            sa = [_dot(_bf(scale(s_ref[b], rows(kk_s, b, t0, j))), bd) for b in range(RW_SEQ)]
            vb = [_dot(vt_s[b], pick) for b in range(RW_SEQ)]
            for b in range(RW_SEQ):
                s = (scale(s_ref[b], rows(w_s, b, t0, j)) - scale(sa[b], rows(kka_s, b, t0, j))
                     + scale(vb[b], rows(k_s, b, t0, j)))
                s_ref[b] = s
                o4 = _dot_nt(head_rows, _bf(side_by_side(scale(s, rows(r_s, b, t0, j)))))
                for h in range(RW_PAIRS):
                    o_s[h, tq, j:j + 1, b * LANES:(b + 1) * LANES] = o4[h:h + 1, :]
        return carry

    lax.fori_loop(0, tb // sub, steps, 0)

    for b in range(RW_SEQ):
        lanes = slice(b * LANES, (b + 1) * LANES)
        nq = tb // sub
        oh = [jnp.concatenate([o_s[h, q, :, lanes] for q in range(nq)], axis=0) if nq > 1
              else o_s[h, 0, :, lanes] for h in range(RW_PAIRS)]
        o = jnp.concatenate([oh[h % RW_PAIRS][:, (h // RW_PAIRS) * RW_DH:(h // RW_PAIRS + 1) * RW_DH]
                             for h in range(RW_HEADS)], axis=-1)
        d = o - _seg_sum(o, bd) * (1.0 / RW_DH)
        var = _seg_sum(d * d, bd) * (1.0 / RW_DH)
        on = d * lax.rsqrt(var + RW_GN_EPS) * gng_ref[...] + gnb_ref[...]
        bonus = _seg_sum(r_s[b] * k_s[b] * rk_ref[...], bd) * v_s[b]
        out_ref[b] = (on + bonus) * g_s[b]

    @pl.when(ti == nt - 1)
    def _():
        for b in range(RW_SEQ):
            for p in pairs:
                s = s_ref[b, p * RW_DH:(p + 1) * RW_DH, :]
                sfin_ref[b, 2 * p] = s[:, 0:RW_DH]
                sfin_ref[b, 2 * p + 1] = s[:, RW_DH:LANES]


def _rwkv(p_rw, shift, s0, mu, w0, wa, a0, g2, k_k, k_a, r_k, gn_g, gn_b, *, tb):
    B, T, _ = p_rw.shape
    kern = functools.partial(_rwkv_kernel, tb=tb)
    vec = lambda n: pl.BlockSpec((1, n), lambda i, j: (0, 0))
    blk = pltpu.VMEM((RW_SEQ, tb, RW_WIDTH), F32)
    oblk = pltpu.VMEM((RW_PAIRS, tb // 8, 8, RW_SEQ * LANES), F32)
    return pl.pallas_call(
        kern,
        out_shape=(jax.ShapeDtypeStruct((B, T, RW_WIDTH), F32),
                   jax.ShapeDtypeStruct((B, RW_HEADS, RW_DH, RW_DH), F32)),
        grid=(B // RW_SEQ, T // tb),
        in_specs=[pl.BlockSpec((RW_SEQ, tb, RW_COLS), lambda i, j: (i, j, 0)),
                  pl.BlockSpec((RW_SEQ, RW_COLS), lambda i, j: (i, 0)),
                  pl.BlockSpec((RW_SEQ, RW_HEADS, RW_DH, RW_DH), lambda i, j: (i, 0, 0, 0)),
                  vec(RW_COLS), vec(RW_WIDTH),
                  pl.BlockSpec((LANES, 2 * RW_WIDTH), lambda i, j: (0, 0)),
                  vec(RW_WIDTH),
                  pl.BlockSpec((LANES, RW_WIDTH), lambda i, j: (0, 0)),
                  vec(RW_WIDTH), vec(RW_WIDTH), vec(RW_WIDTH), vec(RW_WIDTH), vec(RW_WIDTH)],
        out_specs=(pl.BlockSpec((RW_SEQ, tb, RW_WIDTH), lambda i, j: (i, j, 0)),
                   pl.BlockSpec((RW_SEQ, RW_HEADS, RW_DH, RW_DH), lambda i, j: (i, 0, 0, 0))),
        scratch_shapes=[pltpu.VMEM((RW_SEQ, RW_PAIRS * RW_DH, LANES), F32),
                        pltpu.VMEM((RW_SEQ, RW_COLS), F32),
                        blk, blk, blk, blk, blk, blk, blk,
                        pltpu.VMEM((RW_SEQ, RW_PAIRS * RW_DH, LANES), BF16),
                        oblk],
        compiler_params=_params(("parallel", "arbitrary")),
        name="rwkv7_mixer",
    )(p_rw, shift, s0, mu, w0, wa, a0, g2, k_k, k_a, r_k, gn_g, gn_b)


def _layer_norm(y, g, b):
    mu = jnp.mean(y, axis=-1, keepdims=True)
    d = y - mu
    var = jnp.mean(d * d, axis=-1, keepdims=True)
    return d * lax.rsqrt(var + LN_EPS) * g + b


ROW_TILE = 8


def _to_row_tiles(ref, x):
    rows = x.shape[0]
    for s in range(ROW_TILE):
        ref[pl.ds(s, rows, stride=ROW_TILE), :] = x[:, s * LANES:(s + 1) * LANES]


def _from_row_tiles(ref, rows):
    return jnp.concatenate([ref[pl.ds(s, rows, stride=ROW_TILE), :] for s in range(ROW_TILE)], axis=-1)


def _outproj_kernel(x_ref, hg_ref, rw_ref, w_ref, g_ref, b_ref, h_ref, ht_ref):
    mix = (_dot(_bf(hg_ref[...]), w_ref[0:HG_WIDTH, :])
           + _dot(_bf(rw_ref[...]), w_ref[HG_WIDTH:HG_WIDTH + RW_WIDTH, :]))
    h = _layer_norm(ALPHA * x_ref[...] + mix, g_ref[...], b_ref[...])
    h_ref[...] = h
    _to_row_tiles(ht_ref, h)


def _outproj(x2d, hg2d, rw2d, w_out_bf, ln_g, ln_b, tm):
    n = x2d.shape[0]
    rows = lambda w: pl.BlockSpec((tm, w), lambda i: (i, 0))
    return pl.pallas_call(
        _outproj_kernel,
        out_shape=(jax.ShapeDtypeStruct((n, D_MODEL), F32),
                   jax.ShapeDtypeStruct((n * ROW_TILE, LANES), F32)),
        grid=(n // tm,),
        in_specs=[rows(D_MODEL), rows(HG_WIDTH), rows(RW_WIDTH),
                  pl.BlockSpec((D_MODEL, D_MODEL), lambda i: (0, 0)),
                  pl.BlockSpec((1, D_MODEL), lambda i: (0, 0)),
                  pl.BlockSpec((1, D_MODEL), lambda i: (0, 0))],
        out_specs=(rows(D_MODEL), pl.BlockSpec((tm * ROW_TILE, LANES), lambda i: (i, 0))),
        compiler_params=_params(("parallel",)),
        name="out_proj_ln",
    )(x2d, hg2d, rw2d, w_out_bf, ln_g, ln_b)


ROUTE_COLS = 8


def _router_kernel(h_ref, w_ref, eb_ref, eidx_ref, wsel_ref, cnt_ref):
    tm = h_ref.shape[0]
    hh, hl = _split2(h_ref[...])
    wh, wl = _split2(w_ref[...])
    logits = _dot(hh, wh) + (_dot(hh, wl) + _dot(hl, wh))
    lane = lax.broadcasted_iota(jnp.int32, (tm, LANES), 1)
    valid = lane < N_EXPERTS
    neg = -jnp.inf
    scores = _sigmoid(logits)
    choice = jnp.where(valid, scores + eb_ref[...], neg)

    def partner(x, s):
        return jnp.where((lane & s) != 0, pltpu.roll(x, s, axis=1), pltpu.roll(x, LANES - s, axis=1))

    def group_max(x):
        for s in (1, 2, 4):
            x = jnp.maximum(x, partner(x, s))
        return x

    def group_min(x):
        for s in (1, 2, 4):
            x = jnp.minimum(x, partner(x, s))
        return x

    m1 = group_max(choice)
    first = group_min(jnp.where(choice == m1, lane, LANES))
    m2 = group_max(jnp.where(lane == first, neg, choice))
    gscore = m1 + m2
    gper = jnp.where(valid, gscore, pltpu.roll(gscore, N_EXPERTS, axis=1))
    gidx = (lane % N_EXPERTS) // GROUP_SIZE
    rank = jnp.zeros((tm, LANES), jnp.int32)
    for j in range(1, N_GROUPS):
        other = pltpu.roll(gper, GROUP_SIZE * j, axis=1)
        beats = (other > gper) | ((other == gper) & (gidx >= j))
        rank = rank + beats.astype(jnp.int32)
    cand = jnp.where(valid & (rank < TOPK_GROUPS), choice, neg)
    sel = jnp.zeros((tm, LANES), jnp.bool_)
    hits = []
    eidx = jnp.zeros((tm, LANES), jnp.int32)
    for k in range(TOP_K):
        m = jnp.max(cand, axis=-1, keepdims=True)
        idx = jnp.min(jnp.where(cand == m, lane, LANES), axis=-1, keepdims=True)
        hit = lane == idx
        hits.append(hit)
        eidx = jnp.where(lane == k, idx, eidx)
        sel = sel | hit
        cand = jnp.where(hit, neg, cand)
    wts = jnp.where(sel, scores, 0.0)
    gates = wts / jnp.sum(wts, axis=-1, keepdims=True) * ROUTED_SCALE
    wsel = jnp.zeros((tm, LANES), F32)
    for k in range(TOP_K):
        wsel = jnp.where(lane == k, jnp.sum(jnp.where(hits[k], gates, 0.0), axis=-1, keepdims=True), wsel)
    eidx_ref[...] = eidx[:, 0:ROUTE_COLS]
    wsel_ref[...] = wsel[:, 0:ROUTE_COLS]

    @pl.when(pl.program_id(0) == 0)
    def _():
        cnt_ref[...] = jnp.zeros((1, LANES), F32)

    cnt_ref[...] += jnp.sum(jnp.where(sel, 1.0, 0.0), axis=0, keepdims=True)


def _router(h2d, w_router_pad, e_bias_pad, tm):
    n = h2d.shape[0]
    return pl.pallas_call(
        _router_kernel,
        out_shape=(jax.ShapeDtypeStruct((n, ROUTE_COLS), jnp.int32),
                   jax.ShapeDtypeStruct((n, ROUTE_COLS), F32),
                   jax.ShapeDtypeStruct((1, LANES), F32)),
        grid=(n // tm,),
        in_specs=[pl.BlockSpec((tm, D_MODEL), lambda i: (i, 0)),
                  pl.BlockSpec((D_MODEL, LANES), lambda i: (0, 0)),
                  pl.BlockSpec((1, LANES), lambda i: (0, 0))],
        out_specs=(pl.BlockSpec((tm, ROUTE_COLS), lambda i: (i, 0)),
                   pl.BlockSpec((tm, ROUTE_COLS), lambda i: (i, 0)),
                   pl.BlockSpec((1, LANES), lambda i: (0, 0))),
        compiler_params=_params(("arbitrary",)),
        name="router",
    )(h2d, w_router_pad, e_bias_pad)


def _slot_kernel(eidx_ref, off_ref, dest_ref, carry_ref):
    tb = eidx_ref.shape[0]

    @pl.when(pl.program_id(0) == 0)
    def _():
        carry_ref[...] = jnp.zeros((1, LANES), F32)

    lane = lax.broadcasted_iota(jnp.int32, (tb, LANES), 1)
    eidx = eidx_ref[...]
    onehot = [lane == eidx[:, k:k + 1] for k in range(TOP_K)]
    member = jnp.zeros((tb, LANES), F32)
    for k in range(TOP_K):
        member = member + jnp.where(onehot[k], 1.0, 0.0)
    ri = lax.broadcasted_iota(jnp.int32, (tb, tb), 0)
    rj = lax.broadcasted_iota(jnp.int32, (tb, tb), 1)
    before = _bf(jnp.where(rj < ri, 1.0, 0.0))
    slot = _dot(before, _bf(member)) + carry_ref[...] + off_ref[...]
    dest = jnp.zeros((tb, LANES), F32)
    for k in range(TOP_K):
        dest = jnp.where(lane == k, jnp.sum(jnp.where(onehot[k], slot, 0.0), axis=-1, keepdims=True), dest)
    dest_ref[...] = dest[:, 0:ROUTE_COLS].astype(jnp.int32) * ROW_TILE
    carry_ref[...] += jnp.sum(member, axis=0, keepdims=True)


def _slots(eidx, off, tb):
    n = eidx.shape[0]
    return pl.pallas_call(
        _slot_kernel,
        out_shape=jax.ShapeDtypeStruct((n, ROUTE_COLS), jnp.int32),
        grid=(n // tb,),
        in_specs=[pl.BlockSpec((tb, ROUTE_COLS), lambda i: (i, 0)),
                  pl.BlockSpec((1, LANES), lambda i: (0, 0))],
        out_specs=pl.BlockSpec((tb, ROUTE_COLS), lambda i: (i, 0)),
        scratch_shapes=[pltpu.VMEM((1, LANES), F32)],
        compiler_params=_params(("arbitrary",)),
        name="route_slots",
    )(eidx, off)


DMA_GROUP = 4


def _row_copy(src_ref, src_row, dst_ref, dst_row, sem):
    return pltpu.make_async_copy(src_ref.at[pl.ds(pl.multiple_of(src_row, ROW_TILE), ROW_TILE)],
                                 dst_ref.at[pl.ds(pl.multiple_of(dst_row, ROW_TILE), ROW_TILE)], sem)


def _scatter_kernel(dest_ref, h_hbm, xs_in_hbm, xs_hbm, sem, *, tb):
    del xs_in_hbm
    base = pl.program_id(0) * tb

    def start(g, carry):
        r0 = g * DMA_GROUP
        slots = [[dest_ref[(r0 + u) * ROUTE_COLS + k] for k in range(TOP_K)] for u in range(DMA_GROUP)]
        for u in range(DMA_GROUP):
            for k in range(TOP_K):
                _row_copy(h_hbm, (base + r0 + u) * ROW_TILE, xs_hbm, slots[u][k], sem).start()
        return carry

    def wait(r, carry):
        for k in range(TOP_K):
            _row_copy(h_hbm, 0, xs_hbm, 0, sem).wait()
        return carry

    lax.fori_loop(0, tb // DMA_GROUP, start, 0)
    lax.fori_loop(0, tb, wait, 0)


def _scatter_rows(ht, dest_flat, n_slots, tb):
    n = ht.shape[0] // ROW_TILE
    return pl.pallas_call(
        functools.partial(_scatter_kernel, tb=tb),
        out_shape=jax.ShapeDtypeStruct((n_slots * ROW_TILE, LANES), F32),
        grid=(n // tb,),
        in_specs=[pl.BlockSpec((tb * ROUTE_COLS,), lambda i: (i,), memory_space=pltpu.SMEM),
                  pl.BlockSpec(memory_space=pl.ANY),
                  pl.BlockSpec(memory_space=pl.ANY)],
        out_specs=pl.BlockSpec(memory_space=pl.ANY),
        scratch_shapes=[pltpu.SemaphoreType.DMA],
        input_output_aliases={2: 0},
        compiler_params=_params(("arbitrary",)),
        name="route_scatter",
    )(dest_flat, ht, jnp.zeros((n_slots * ROW_TILE, LANES), F32))


def _experts_kernel(te_ref, nt_ref, xs_ref, wg_ref, wu_ref, wd_ref, ys_ref):
    del te_ref
    live = pl.program_id(0) < nt_ref[0]

    @pl.when(live)
    def _():
        tm = xs_ref.shape[0] // ROW_TILE
        xb = _bf(_from_row_tiles(xs_ref, tm))
        act = _silu(_dot(xb, _bf(wg_ref[0]))) * _dot(xb, _bf(wu_ref[0]))
        _to_row_tiles(ys_ref, _dot(_bf(act), _bf(wd_ref[0])))

    @pl.when(jnp.logical_not(live))
    def _():
        ys_ref[...] = jnp.zeros(ys_ref.shape, F32)


def _experts(xs, tile_expert, n_tiles, w_gate, w_up, w_down, tm):
    n_slots = xs.shape[0] // ROW_TILE
    return pl.pallas_call(
        _experts_kernel,
        out_shape=jax.ShapeDtypeStruct((n_slots * ROW_TILE, LANES), F32),
        grid_spec=pltpu.PrefetchScalarGridSpec(
            num_scalar_prefetch=2,
            grid=(n_slots // tm,),
            in_specs=[pl.BlockSpec((tm * ROW_TILE, LANES), lambda g, te, nt: (g, 0)),
                      pl.BlockSpec((1, D_MODEL, D_EXPERT), lambda g, te, nt: (te[g], 0, 0)),
                      pl.BlockSpec((1, D_MODEL, D_EXPERT), lambda g, te, nt: (te[g], 0, 0)),
                      pl.BlockSpec((1, D_EXPERT, D_MODEL), lambda g, te, nt: (te[g], 0, 0))],
            out_specs=pl.BlockSpec((tm * ROW_TILE, LANES), lambda g, te, nt: (g, 0))),
        compiler_params=_params(("arbitrary",)),
        name="routed_experts",
    )(tile_expert, n_tiles, xs, w_gate, w_up, w_down)


def _combine_kernel(dest_ref, dnext_ref, h_ref, wsel_ref, sg_ref, su_ref, sd_ref, g_ref, b_ref, ys_hbm,
                    out_ref, yg_ref, sem, *, tc):
    i = pl.program_id(0)
    last = pl.num_programs(0) - 1

    def fetch(d_ref, buf, wait):
        def body(g, carry):
            r0 = g * DMA_GROUP
            slots = [[0 if wait else d_ref[(r0 + u) * ROUTE_COLS + k] for k in range(TOP_K)]
                     for u in range(DMA_GROUP)]
            for u in range(DMA_GROUP):
                for k in range(TOP_K):
                    cp = _row_copy(ys_hbm, slots[u][k], yg_ref.at[buf, k], (r0 + u) * ROW_TILE, sem.at[buf])
                    cp.wait() if wait else cp.start()
            return carry
        lax.fori_loop(0, tc // DMA_GROUP, body, 0)

    cur = i % 2

    @pl.when(i == 0)
    def _():
        fetch(dest_ref, 0, False)

    @pl.when(jnp.logical_and(i < last, cur == 0))
    def _():
        fetch(dnext_ref, 1, False)

    @pl.when(jnp.logical_and(i < last, cur == 1))
    def _():
        fetch(dnext_ref, 0, False)

    h = h_ref[...]
    hb = _bf(h)
    shared = _dot(_bf(_silu(_dot(hb, sg_ref[...])) * _dot(hb, su_ref[...])), sd_ref[...])
    wsel = wsel_ref[...]

    def finish(buf):
        fetch(dest_ref, buf, True)
        acc = ALPHA * h + shared
        for k in range(TOP_K):
            acc = acc + wsel[:, k:k + 1] * _from_row_tiles(yg_ref.at[buf, k], tc)
        out_ref[...] = _layer_norm(acc, g_ref[...], b_ref[...])

    @pl.when(cur == 0)
    def _():
        finish(0)

    @pl.when(cur == 1)
    def _():
        finish(1)


def _combine(h2d, ys, dest_flat, wsel, sg_bf, su_bf, sd_bf, ln_g, ln_b, tc):
    n = h2d.shape[0]
    nblk = n // tc
    const = lambda shape: pl.BlockSpec(shape, lambda i: (0,) * len(shape))
    idx = lambda f: pl.BlockSpec((tc * ROUTE_COLS,), f, memory_space=pltpu.SMEM)
    return pl.pallas_call(
        functools.partial(_combine_kernel, tc=tc),
        out_shape=jax.ShapeDtypeStruct((n, D_MODEL), F32),
        grid=(nblk,),
        in_specs=[idx(lambda i: (i,)), idx(lambda i: (jnp.minimum(i + 1, nblk - 1),)),
                  pl.BlockSpec((tc, D_MODEL), lambda i: (i, 0)),
                  pl.BlockSpec((tc, ROUTE_COLS), lambda i: (i, 0)),
                  const((D_MODEL, D_EXPERT)), const((D_MODEL, D_EXPERT)), const((D_EXPERT, D_MODEL)),
                  const((1, D_MODEL)), const((1, D_MODEL)),
                  pl.BlockSpec(memory_space=pl.ANY)],
        out_specs=pl.BlockSpec((tc, D_MODEL), lambda i: (i, 0)),
        scratch_shapes=[pltpu.VMEM((2, TOP_K, tc * ROW_TILE, LANES), F32), pltpu.SemaphoreType.DMA((2,))],
        compiler_params=_params(("arbitrary",)),
        name="route_combine",
    )(dest_flat, dest_flat, h2d, wsel, sg_bf, su_bf, sd_bf, ln_g, ln_b, ys)


def _moe(h2d, ht, w, *, tm):
    n = h2d.shape[0]
    n_tiles_max = (n * TOP_K) // tm + N_EXPERTS
    eidx, wsel, counts = _router(h2d, w["w_router"], w["e_bias"], _row_tile(n, 512))
    cnt = counts[0, :N_EXPERTS].astype(jnp.int32)
    tiles = (cnt + tm - 1) // tm
    tile_end = jnp.cumsum(tiles)
    off = ((tile_end - tiles) * tm).astype(F32)
    tile_expert = jnp.minimum(
        jnp.sum(tile_end[None, :] <= jnp.arange(n_tiles_max, dtype=jnp.int32)[:, None], axis=1),
        N_EXPERTS - 1).astype(jnp.int32)
    dest = _slots(eidx, jnp.pad(off, (0, LANES - N_EXPERTS))[None, :], _row_tile(n, 256))
    dest_flat = dest.reshape(n * ROUTE_COLS)
    xs = _scatter_rows(ht, dest_flat, n_tiles_max * tm, _row_tile(n, 128))
    ys = _experts(xs, tile_expert, tile_end[N_EXPERTS - 1:], w["w_gate"], w["w_up"], w["w_down"], tm)
    return _combine(h2d, ys, dest_flat, wsel, w["ws_gate"], w["ws_up"], w["ws_down"],
                    w["ln2_g"], w["ln2_b"], _row_tile(n, 128))


def _row_tile(n, cap):
    t = cap
    while n % t:
        t //= 2
    return t


def _trunk(x, s_hg, s_rw, shift, w, *, hg_cfg, rw_tb, moe_tm):
    B, T, _ = x.shape
    n = B * T
    x2d = x.reshape(n, D_MODEL)
    p_hg, p_rw = _proj(x2d, w["w_in"], _row_tile(n, 512))
    p_hg = p_hg.reshape(B, T, HG_COLS)
    p_rw = p_rw.reshape(B, T, RW_COLS)
    hg_out, hg_state = _hgrn(p_hg, s_hg, w["hg_lb"], w["hg_norm_g"], **hg_cfg)
    rw_out, rw_state = _rwkv(p_rw, shift, s_rw, w["rw_mu"], w["rw_w0"], w["rw_wa"], w["rw_a0"],
                             w["rw_g2"], w["rw_k_k"], w["rw_k_a"], w["rw_r_k"], w["rw_gn_g"],
                             w["rw_gn_b"], tb=rw_tb)
    h, ht = _outproj(x2d, hg_out.reshape(n, HG_WIDTH), rw_out.reshape(n, RW_WIDTH), w["w_out"],
                     w["ln1_g"], w["ln1_b"], _row_tile(n, 512))
    y = _moe(h, ht, w, tm=moe_tm)
    return y.reshape(B, T, D_MODEL), hg_state, rw_state, p_rw[:, T - 1, :]


def kernel(x_prompt, x_sample, state_hgrn, state_rwkv, state_shift, w_in, w_out, hg_lb, hg_norm_g, rw_mu, rw_w0, rw_w2, rw_a0, rw_a2, rw_g2, rw_k_k, rw_k_a, rw_r_k, rw_gn_g, rw_gn_b, ln1_g, ln1_b, ln2_g, ln2_b, w_router, e_bias, w_gate, w_up, w_down, ws_gate, ws_up, ws_down):
    l = 0
    zeros_lora = jnp.zeros((RW_DH, RW_WIDTH), F32)
    w = {
        "w_in": _bf(w_in[l]), "w_out": _bf(w_out[l]),
        "hg_lb": hg_lb, "hg_norm_g": hg_norm_g[l][None, :],
        "rw_mu": rw_mu[l][None, :], "rw_w0": rw_w0[l][None, :], "rw_a0": rw_a0[l][None, :],
        "rw_wa": _bf(jnp.concatenate([jnp.concatenate([rw_w2[l], zeros_lora], axis=1),
                                      jnp.concatenate([zeros_lora, rw_a2[l]], axis=1)], axis=0)),
        "rw_g2": _bf(rw_g2[l]),
        "rw_k_k": rw_k_k[l][None, :], "rw_k_a": rw_k_a[l][None, :],
        "rw_r_k": rw_r_k[l].reshape(1, RW_WIDTH),
        "rw_gn_g": rw_gn_g[l][None, :], "rw_gn_b": rw_gn_b[l][None, :],
        "ln1_g": ln1_g[l][None, :], "ln1_b": ln1_b[l][None, :],
        "ln2_g": ln2_g[l][None, :], "ln2_b": ln2_b[l][None, :],
        "w_router": jnp.pad(w_router[l], ((0, 0), (0, LANES - N_EXPERTS))),
        "e_bias": jnp.pad(e_bias[l], (0, LANES - N_EXPERTS))[None, :],
        "w_gate": w_gate[l], "w_up": w_up[l], "w_down": w_down[l],
        "ws_gate": _bf(ws_gate[l]), "ws_up": _bf(ws_up[l]), "ws_down": _bf(ws_down[l]),
    }
    Bp = x_prompt.shape[0]
    zero_hg = jnp.zeros((Bp, HG_HEADS, HG_DK, HG_DK), F32)
    zero_rw = jnp.zeros((Bp, RW_HEADS, RW_DH, RW_DH), F32)
    zero_sh = jnp.zeros((Bp, RW_COLS), F32)
    yp, hg_p, rw_p, sh_p = _trunk(x_prompt, zero_hg, zero_rw, zero_sh, w,
                                  hg_cfg=dict(Bb=1, Tb=128, C=64, R=16), rw_tb=64, moe_tm=256)
    ys, hg_s, rw_s, sh_s = _trunk(x_sample, state_hgrn[l], state_rwkv[l], state_shift[l], w,
                                  hg_cfg=dict(Bb=8, Tb=8, C=8, R=8), rw_tb=8, moe_tm=128)
    return (yp, ys, hg_p[None], rw_p[None], sh_p[None], hg_s[None], rw_s[None], sh_s[None])
```

```python
import functools

import jax
import jax.numpy as jnp
from jax import lax
from jax.experimental import pallas as pl
from jax.experimental.pallas import tpu as pltpu

F32 = jnp.float32
BF16 = jnp.bfloat16

D_MODEL = 1024
HG_HEADS = 4
HG_DK = 128
HG_WIDTH = 512
HG_COLS = 2048
RW_HEADS = 8
RW_DH = 64
RW_WIDTH = 512
RW_COLS = 1792
N_EXPERTS = 64
TOP_K = 6
N_GROUPS = 8
TOPK_GROUPS = 4
GROUP_SIZE = N_EXPERTS // N_GROUPS
D_EXPERT = 256
ROUTED_SCALE = 2.5
DEPTH = 1
ALPHA = (2.0 * DEPTH) ** 0.25
LN_EPS = 1e-5
HEAD_NORM_EPS = 1e-6
RW_GN_EPS = 64e-5

LANES = 128
VMEM_LIMIT = 56 * 1024 * 1024


def _bf(x):
    return x.astype(BF16)


def _dot(a, b):
    return jnp.dot(a, b, preferred_element_type=F32)


def _dot_nt(a, b):
    return lax.dot_general(a, b, (((1,), (1,)), ((), ())), preferred_element_type=F32)


def _split2(x):
    hi = _bf(x)
    lo = _bf(x - hi.astype(F32))
    return hi, lo


def _split3(x):
    hi = _bf(x)
    r1 = x - hi.astype(F32)
    mid = _bf(r1)
    lo = _bf(r1 - mid.astype(F32))
    return hi, mid, lo


def _sigmoid(x):
    return 1.0 / (1.0 + jnp.exp(-x))


def _silu(x):
    return x * _sigmoid(x)


def _params(sem):
    return pltpu.CompilerParams(dimension_semantics=sem, vmem_limit_bytes=VMEM_LIMIT)


def _proj_kernel(x_ref, w_ref, hg_ref, rw_ref):
    xb = _bf(x_ref[...])
    step = 256
    for c0 in range(0, HG_COLS, step):
        hg_ref[:, c0:c0 + step] = _dot(xb, w_ref[:, c0:c0 + step])
    for c0 in range(0, RW_COLS, step):
        rw_ref[:, c0:c0 + step] = _dot(xb, w_ref[:, HG_COLS + c0:HG_COLS + c0 + step])


def _proj(x2d, w_in_bf, tm):
    n = x2d.shape[0]
    return pl.pallas_call(
        _proj_kernel,
        out_shape=(jax.ShapeDtypeStruct((n, HG_COLS), F32), jax.ShapeDtypeStruct((n, RW_COLS), F32)),
        grid=(n // tm,),
        in_specs=[pl.BlockSpec((tm, D_MODEL), lambda i: (i, 0)),
                  pl.BlockSpec((D_MODEL, HG_COLS + RW_COLS), lambda i: (0, 0))],
        out_specs=(pl.BlockSpec((tm, HG_COLS), lambda i: (i, 0)),
                   pl.BlockSpec((tm, RW_COLS), lambda i: (i, 0))),
        compiler_params=_params(("parallel",)),
        name="in_proj",
    )(x2d, w_in_bf)


def _hgrn_kernel(p_ref, s0_ref, lbraw_ref, ng_ref, out_ref, sfin_ref, st_ref, *, Bb, Tb, C, R):
    ti = pl.program_id(1)
    nt = pl.num_programs(1)
    nb = C // R

    @pl.when(ti == 0)
    def _():
        for b in range(Bb):
            for h in range(HG_HEADS):
                st_ref[b, h] = s0_ref[b, h].T

    lbr = lbraw_ref[...]
    lbe = jnp.exp(lbr - jnp.max(lbr, axis=0, keepdims=True))
    lb = lbe[0:1, :] / jnp.sum(lbe, axis=0, keepdims=True)
    ng = ng_ref[...]

    if Tb > 8:
        rows = lax.broadcasted_iota(jnp.int32, (Tb, Tb), 0)
        cols = lax.broadcasted_iota(jnp.int32, (Tb, Tb), 1)
        tril = _bf(jnp.where(((rows // C) == (cols // C)) & (cols <= rows), 1.0, 0.0))
    tidx = lax.broadcasted_iota(jnp.int32, (R, 1), 0)
    cidx = lax.broadcasted_iota(jnp.int32, (C, 1), 0)

    for b in range(Bb):
        qs = _silu(p_ref[b, :, 0:512])
        f = lb + (1.0 - lb) * _sigmoid(p_ref[b, :, 512:1024])
        lf = jnp.log(f)
        kd = 1.0 - f
        v = p_ref[b, :, 1024:1536]
        if Tb == 8:
            r8 = lax.broadcasted_iota(jnp.int32, (8, 1), 0)
            bcum = jnp.concatenate(
                [jnp.sum(jnp.where(r8 <= t, lf, 0.0), axis=0, keepdims=True) for t in range(8)], axis=0)
        else:
            hi, mid, lo = _split3(lf)
            bcum = _dot(tril, hi) + _dot(tril, mid) + _dot(tril, lo)
        o_heads = []
        for h in range(HG_HEADS):
            sl = slice(h * HG_DK, (h + 1) * HG_DK)
            st = st_ref[b, h]
            o_chunks = []
            for c in range(Tb // C):
                r0 = c * C
                bc = bcum[r0:r0 + C, sl]
                qc = qs[r0:r0 + C, sl]
                kc = kd[r0:r0 + C, sl]
                vc = v[r0:r0 + C, sl]
                b_last = bc[C - 1:C, :]
                o_inter = _dot_nt(_bf(qc * jnp.exp(bc)), _bf(st))
                o_blocks = []
                for i in range(nb):
                    rs = slice(i * R, (i + 1) * R)
                    bq = bc[rs]
                    qq = qc[rs]
                    acc = jnp.zeros((R, HG_DK), F32)
                    if i > 0:
                        anchor = bc[i * R - 1:i * R, :]
                        early = cidx < i * R
                        qa = qq * jnp.exp(bq - anchor)
                        ka = jnp.where(early, kc * jnp.exp(jnp.where(early, anchor - bc, 0.0)), 0.0)
                        att = _dot_nt(_bf(qa), _bf(ka))
                        acc = acc + _dot(_bf(att), _bf(vc))
                    for s in range(R):
                        sa = i * R + s
                        msk = tidx >= s
                        e = jnp.exp(jnp.where(msk, bq - bc[sa:sa + 1, :], 0.0))
                        term = jnp.where(msk, qq * kc[sa:sa + 1, :] * e, 0.0)
                        acc = acc + jnp.sum(term, axis=-1, keepdims=True) * vc[sa:sa + 1, :]
                    o_blocks.append(acc)
                o_intra = o_blocks[0] if nb == 1 else jnp.concatenate(o_blocks, axis=0)
                o_chunks.append(o_inter + o_intra)
                k_out = kc * jnp.exp(b_last - bc)
                st = st * jnp.exp(b_last) + _dot(_bf(vc.T), _bf(k_out))
            st_ref[b, h] = st
            o_h = o_chunks[0] if len(o_chunks) == 1 else jnp.concatenate(o_chunks, axis=0)
            o_h = o_h * lax.rsqrt(jnp.mean(o_h * o_h, axis=-1, keepdims=True) + HEAD_NORM_EPS) * ng[:, sl]
            o_heads.append(o_h)
        out_ref[b] = jnp.concatenate(o_heads, axis=-1) * _silu(p_ref[b, :, 1536:2048])

    @pl.when(ti == nt - 1)
    def _():
        for b in range(Bb):
            for h in range(HG_HEADS):
                sfin_ref[b, h] = st_ref[b, h].T


def _hgrn(p_hg, s0, lb_raw, norm_g, *, Bb, Tb, C, R):
    B, T, _ = p_hg.shape
    kern = functools.partial(_hgrn_kernel, Bb=Bb, Tb=Tb, C=C, R=R)
    return pl.pallas_call(
        kern,
        out_shape=(jax.ShapeDtypeStruct((B, T, HG_WIDTH), F32),
                   jax.ShapeDtypeStruct((B, HG_HEADS, HG_DK, HG_DK), F32)),
        grid=(B // Bb, T // Tb),
        in_specs=[pl.BlockSpec((Bb, Tb, HG_COLS), lambda i, j: (i, j, 0)),
                  pl.BlockSpec((Bb, HG_HEADS, HG_DK, HG_DK), lambda i, j: (i, 0, 0, 0)),
                  pl.BlockSpec((DEPTH + 1, HG_WIDTH), lambda i, j: (0, 0)),
                  pl.BlockSpec((1, HG_WIDTH), lambda i, j: (0, 0))],
        out_specs=(pl.BlockSpec((Bb, Tb, HG_WIDTH), lambda i, j: (i, j, 0)),
                   pl.BlockSpec((Bb, HG_HEADS, HG_DK, HG_DK), lambda i, j: (i, 0, 0, 0))),
        scratch_shapes=[pltpu.VMEM((Bb, HG_HEADS, HG_DK, HG_DK), F32)],
        compiler_params=_params(("parallel", "arbitrary")),
        name="hgrn2_mixer",
    )(p_hg, s0, lb_raw, norm_g)


RW_PAIRS = RW_HEADS // 2
EXP_NEG_HALF = 0.6065306597126334
RW_SEQ = 8


def _block_ones():
    i = lax.broadcasted_iota(jnp.int32, (LANES, LANES), 0)
    j = lax.broadcasted_iota(jnp.int32, (LANES, LANES), 1)
    return _bf(jnp.where((i // RW_DH) == (j // RW_DH), 1.0, 0.0))


def _seg_sum(x, bd):
    outs = []
    for c in range(RW_WIDTH // LANES):
        hi, lo = _split2(x[:, c * LANES:(c + 1) * LANES])
        outs.append(_dot(hi, bd) + _dot(lo, bd))
    return jnp.concatenate(outs, axis=-1)


def _rwkv_kernel(p_ref, shift_ref, s0_ref, mu_ref, w0_ref, wa_ref, a0_ref, g2_ref, kk_ref, ka_ref,
                 rk_ref, gng_ref, gnb_ref, out_ref, sfin_ref,
                 s_ref, prev_ref, r_s, w_s, k_s, v_s, kk_s, kka_s, g_s, vt_s, o_s, *, tb):
    ti = pl.program_id(1)
    nt = pl.num_programs(1)
    pairs = range(RW_PAIRS)
    sub = 8

    @pl.when(ti == 0)
    def _():
        prev_ref[...] = shift_ref[...]
        for b in range(RW_SEQ):
            for p in pairs:
                s_ref[b, p * RW_DH:(p + 1) * RW_DH, :] = jnp.concatenate(
                    [s0_ref[b, 2 * p], s0_ref[b, 2 * p + 1]], axis=-1)

    bd = _block_ones()
    lane = lax.broadcasted_iota(jnp.int32, (1, LANES), 1)
    rid = lax.broadcasted_iota(jnp.int32, (tb, 1), 0)
    mu = mu_ref[...]

    for b in range(RW_SEQ):
        p = p_ref[b]
        prev = jnp.where(rid == 0, prev_ref[b:b + 1, :], pltpu.roll(p, 1, axis=0))
        prev_ref[b:b + 1, :] = p[tb - 1:tb, :]
        ps = p + mu * (prev - p)
        r = ps[:, 0:512]
        k = ps[:, 512:1024]
        v = ps[:, 1024:1536]
        zz = ps[:, 1536:1664]
        zg = ps[:, 1664:1792]
        lora = _dot(_bf(jnp.where(lane < 64, jnp.tanh(zz), zz)), wa_ref[...])
        exp_w = EXP_NEG_HALF / (1.0 + jnp.exp(-(w0_ref[...] + lora[:, 0:512])))
        a = _sigmoid(a0_ref[...] + lora[:, 512:1024])
        kk = k * kk_ref[...]
        kk = kk / jnp.maximum(jnp.sqrt(_seg_sum(kk * kk, bd)), 1e-12)
        r_s[b] = r
        w_s[b] = jnp.exp(-exp_w)
        k_s[b] = k * (1.0 + (a - 1.0) * ka_ref[...])
        v_s[b] = v
        kk_s[b] = kk
        kka_s[b] = kk * a
        g_s[b] = _dot(_bf(_sigmoid(zg)), g2_ref[...])
        for q in pairs:
            vp = v[:, q * LANES:(q + 1) * LANES]
            if tb < RW_DH:
                vp = jnp.concatenate([vp, jnp.zeros((RW_DH - tb, LANES), F32)], axis=0)
            vpt = vp.T
            vt_s[b, q * RW_DH:(q + 1) * RW_DH, :] = _bf(
                jnp.concatenate([vpt[0:RW_DH, :], vpt[RW_DH:LANES, :]], axis=1))

    ci = lax.broadcasted_iota(jnp.int32, (LANES, LANES), 0)
    cj = lax.broadcasted_iota(jnp.int32, (LANES, LANES), 1)
    same_head = (ci // RW_DH) == (cj // RW_DH)
    si = lax.broadcasted_iota(jnp.int32, (16, 2 * LANES), 0)
    sj = lax.broadcasted_iota(jnp.int32, (16, 2 * LANES), 1)
    head_rows = _bf(jnp.where(si == sj // RW_DH, 1.0, 0.0))

    def rows(ref, b, t0, j):
        return [ref[b, pl.ds(t0, sub), pl.ds(q * LANES, LANES)][j:j + 1, :] for q in pairs]

    def scale(x, rws):
        return jnp.concatenate([x[q * RW_DH:(q + 1) * RW_DH] * rws[q] for q in pairs], axis=0)

    def side_by_side(x):
        return jnp.concatenate(
            [jnp.concatenate([x[0:RW_DH], x[RW_DH:2 * RW_DH]], axis=1),
             jnp.concatenate([x[2 * RW_DH:3 * RW_DH], x[3 * RW_DH:4 * RW_DH]], axis=1)], axis=0)

    def steps(tq, carry):
        t0 = pl.multiple_of(tq * sub, sub)
        for j in range(sub):
            tl = tq * sub + j
            pick = _bf(jnp.where(same_head & ((ci % RW_DH) == tl), 1.0, 0.0))
            sa = [_dot(_bf(scale(s_ref[b], rows(kk_s, b, t0, j))), bd) for b in range(RW_SEQ)]
            vb = [_dot(vt_s[b], pick) for b in range(RW_SEQ)]
            for b in range(RW_SEQ):
                s = (scale(s_ref[b], rows(w_s, b, t0, j)) - scale(sa[b], rows(kka_s, b, t0, j))
                     + scale(vb[b], rows(k_s, b, t0, j)))
                s_ref[b] = s
                o4 = _dot_nt(head_rows, _bf(side_by_side(scale(s, rows(r_s, b, t0, j)))))
                for h in range(RW_PAIRS):
                    o_s[h, tq, j:j + 1, b * LANES:(b + 1) * LANES] = o4[h:h + 1, :]
        return carry

    lax.fori_loop(0, tb // sub, steps, 0)

    for b in range(RW_SEQ):
        lanes = slice(b * LANES, (b + 1) * LANES)
        nq = tb // sub
        oh = [jnp.concatenate([o_s[h, q, :, lanes] for q in range(nq)], axis=0) if nq > 1
              else o_s[h, 0, :, lanes] for h in range(RW_PAIRS)]
        o = jnp.concatenate([oh[h % RW_PAIRS][:, (h // RW_PAIRS) * RW_DH:(h // RW_PAIRS + 1) * RW_DH]
                             for h in range(RW_HEADS)], axis=-1)
        d = o - _seg_sum(o, bd) * (1.0 / RW_DH)
        var = _seg_sum(d * d, bd) * (1.0 / RW_DH)
        on = d * lax.rsqrt(var + RW_GN_EPS) * gng_ref[...] + gnb_ref[...]
        bonus = _seg_sum(r_s[b] * k_s[b] * rk_ref[...], bd) * v_s[b]
        out_ref[b] = (on + bonus) * g_s[b]

    @pl.when(ti == nt - 1)
    def _():
        for b in range(RW_SEQ):
            for p in pairs:
                s = s_ref[b, p * RW_DH:(p + 1) * RW_DH, :]
                sfin_ref[b, 2 * p] = s[:, 0:RW_DH]
                sfin_ref[b, 2 * p + 1] = s[:, RW_DH:LANES]


def _rwkv(p_rw, shift, s0, mu, w0, wa, a0, g2, k_k, k_a, r_k, gn_g, gn_b, *, tb):
    B, T, _ = p_rw.shape
    kern = functools.partial(_rwkv_kernel, tb=tb)
    vec = lambda n: pl.BlockSpec((1, n), lambda i, j: (0, 0))
    blk = pltpu.VMEM((RW_SEQ, tb, RW_WIDTH), F32)
    oblk = pltpu.VMEM((RW_PAIRS, tb // 8, 8, RW_SEQ * LANES), F32)
    return pl.pallas_call(
        kern,
        out_shape=(jax.ShapeDtypeStruct((B, T, RW_WIDTH), F32),
                   jax.ShapeDtypeStruct((B, RW_HEADS, RW_DH, RW_DH), F32)),
        grid=(B // RW_SEQ, T // tb),
        in_specs=[pl.BlockSpec((RW_SEQ, tb, RW_COLS), lambda i, j: (i, j, 0)),
                  pl.BlockSpec((RW_SEQ, RW_COLS), lambda i, j: (i, 0)),
                  pl.BlockSpec((RW_SEQ, RW_HEADS, RW_DH, RW_DH), lambda i, j: (i, 0, 0, 0)),
                  vec(RW_COLS), vec(RW_WIDTH),
                  pl.BlockSpec((LANES, 2 * RW_WIDTH), lambda i, j: (0, 0)),
                  vec(RW_WIDTH),
                  pl.BlockSpec((LANES, RW_WIDTH), lambda i, j: (0, 0)),
                  vec(RW_WIDTH), vec(RW_WIDTH), vec(RW_WIDTH), vec(RW_WIDTH), vec(RW_WIDTH)],
        out_specs=(pl.BlockSpec((RW_SEQ, tb, RW_WIDTH), lambda i, j: (i, j, 0)),
                   pl.BlockSpec((RW_SEQ, RW_HEADS, RW_DH, RW_DH), lambda i, j: (i, 0, 0, 0))),
        scratch_shapes=[pltpu.VMEM((RW_SEQ, RW_PAIRS * RW_DH, LANES), F32),
                        pltpu.VMEM((RW_SEQ, RW_COLS), F32),
                        blk, blk, blk, blk, blk, blk, blk,
                        pltpu.VMEM((RW_SEQ, RW_PAIRS * RW_DH, LANES), BF16),
                        oblk],
        compiler_params=_params(("parallel", "arbitrary")),
        name="rwkv7_mixer",
    )(p_rw, shift, s0, mu, w0, wa, a0, g2, k_k, k_a, r_k, gn_g, gn_b)


def _layer_norm(y, g, b):
    mu = jnp.mean(y, axis=-1, keepdims=True)
    d = y - mu
    var = jnp.mean(d * d, axis=-1, keepdims=True)
    return d * lax.rsqrt(var + LN_EPS) * g + b


ROW_TILE = 8


def _to_row_tiles(ref, x):
    rows = x.shape[0]
    for s in range(ROW_TILE):
        ref[pl.ds(s, rows, stride=ROW_TILE), :] = x[:, s * LANES:(s + 1) * LANES]


def _from_row_tiles(ref, rows):
    return jnp.concatenate([ref[pl.ds(s, rows, stride=ROW_TILE), :] for s in range(ROW_TILE)], axis=-1)


def _outproj_kernel(x_ref, hg_ref, rw_ref, w_ref, g_ref, b_ref, h_ref, ht_ref):
    mix = (_dot(_bf(hg_ref[...]), w_ref[0:HG_WIDTH, :])
           + _dot(_bf(rw_ref[...]), w_ref[HG_WIDTH:HG_WIDTH + RW_WIDTH, :]))
    h = _layer_norm(ALPHA * x_ref[...] + mix, g_ref[...], b_ref[...])
    h_ref[...] = h
    _to_row_tiles(ht_ref, h)


def _outproj(x2d, hg2d, rw2d, w_out_bf, ln_g, ln_b, tm):
    n = x2d.shape[0]
    rows = lambda w: pl.BlockSpec((tm, w), lambda i: (i, 0))
    return pl.pallas_call(
        _outproj_kernel,
        out_shape=(jax.ShapeDtypeStruct((n, D_MODEL), F32),
                   jax.ShapeDtypeStruct((n * ROW_TILE, LANES), F32)),
        grid=(n // tm,),
        in_specs=[rows(D_MODEL), rows(HG_WIDTH), rows(RW_WIDTH),
                  pl.BlockSpec((D_MODEL, D_MODEL), lambda i: (0, 0)),
                  pl.BlockSpec((1, D_MODEL), lambda i: (0, 0)),
                  pl.BlockSpec((1, D_MODEL), lambda i: (0, 0))],
        out_specs=(rows(D_MODEL), pl.BlockSpec((tm * ROW_TILE, LANES), lambda i: (i, 0))),
        compiler_params=_params(("parallel",)),
        name="out_proj_ln",
    )(x2d, hg2d, rw2d, w_out_bf, ln_g, ln_b)


ROUTE_COLS = 8


def _router_kernel(h_ref, w_ref, eb_ref, eidx_ref, wsel_ref, cnt_ref):
    tm = h_ref.shape[0]
    hh, hl = _split2(h_ref[...])
    wh, wl = _split2(w_ref[...])
    logits = _dot(hh, wh) + (_dot(hh, wl) + _dot(hl, wh))
    lane = lax.broadcasted_iota(jnp.int32, (tm, LANES), 1)
    valid = lane < N_EXPERTS
    neg = -jnp.inf
    scores = _sigmoid(logits)
    choice = jnp.where(valid, scores + eb_ref[...], neg)

    def partner(x, s):
        return jnp.where((lane & s) != 0, pltpu.roll(x, s, axis=1), pltpu.roll(x, LANES - s, axis=1))

    def group_max(x):
        for s in (1, 2, 4):
            x = jnp.maximum(x, partner(x, s))
        return x

    def group_min(x):
        for s in (1, 2, 4):
            x = jnp.minimum(x, partner(x, s))
        return x

    m1 = group_max(choice)
    first = group_min(jnp.where(choice == m1, lane, LANES))
    m2 = group_max(jnp.where(lane == first, neg, choice))
    gscore = m1 + m2
    gper = jnp.where(valid, gscore, pltpu.roll(gscore, N_EXPERTS, axis=1))
    gidx = (lane % N_EXPERTS) // GROUP_SIZE
    rank = jnp.zeros((tm, LANES), jnp.int32)
    for j in range(1, N_GROUPS):
        other = pltpu.roll(gper, GROUP_SIZE * j, axis=1)
        beats = (other > gper) | ((other == gper) & (gidx >= j))
        rank = rank + beats.astype(jnp.int32)
    cand = jnp.where(valid & (rank < TOPK_GROUPS), choice, neg)
    sel = jnp.zeros((tm, LANES), jnp.bool_)
    hits = []
    eidx = jnp.zeros((tm, LANES), jnp.int32)
    for k in range(TOP_K):
        m = jnp.max(cand, axis=-1, keepdims=True)
        idx = jnp.min(jnp.where(cand == m, lane, LANES), axis=-1, keepdims=True)
        hit = lane == idx
        hits.append(hit)
        eidx = jnp.where(lane == k, idx, eidx)
        sel = sel | hit
        cand = jnp.where(hit, neg, cand)
    wts = jnp.where(sel, scores, 0.0)
    gates = wts / jnp.sum(wts, axis=-1, keepdims=True) * ROUTED_SCALE
    wsel = jnp.zeros((tm, LANES), F32)
    for k in range(TOP_K):
        wsel = jnp.where(lane == k, jnp.sum(jnp.where(hits[k], gates, 0.0), axis=-1, keepdims=True), wsel)
    eidx_ref[...] = eidx[:, 0:ROUTE_COLS]
    wsel_ref[...] = wsel[:, 0:ROUTE_COLS]

    @pl.when(pl.program_id(0) == 0)
    def _():
        cnt_ref[...] = jnp.zeros((1, LANES), F32)

    cnt_ref[...] += jnp.sum(jnp.where(sel, 1.0, 0.0), axis=0, keepdims=True)


def _router(h2d, w_router_pad, e_bias_pad, tm):
    n = h2d.shape[0]
    return pl.pallas_call(
        _router_kernel,
        out_shape=(jax.ShapeDtypeStruct((n, ROUTE_COLS), jnp.int32),
                   jax.ShapeDtypeStruct((n, ROUTE_COLS), F32),
                   jax.ShapeDtypeStruct((1, LANES), F32)),
        grid=(n // tm,),
        in_specs=[pl.BlockSpec((tm, D_MODEL), lambda i: (i, 0)),
                  pl.BlockSpec((D_MODEL, LANES), lambda i: (0, 0)),
                  pl.BlockSpec((1, LANES), lambda i: (0, 0))],
        out_specs=(pl.BlockSpec((tm, ROUTE_COLS), lambda i: (i, 0)),
                   pl.BlockSpec((tm, ROUTE_COLS), lambda i: (i, 0)),
                   pl.BlockSpec((1, LANES), lambda i: (0, 0))),
        compiler_params=_params(("arbitrary",)),
        name="router",
    )(h2d, w_router_pad, e_bias_pad)


def _slot_kernel(eidx_ref, off_ref, dest_ref, carry_ref):
    tb = eidx_ref.shape[0]

    @pl.when(pl.program_id(0) == 0)
    def _():
        carry_ref[...] = jnp.zeros((1, LANES), F32)

    lane = lax.broadcasted_iota(jnp.int32, (tb, LANES), 1)
    eidx = eidx_ref[...]
    onehot = [lane == eidx[:, k:k + 1] for k in range(TOP_K)]
    member = jnp.zeros((tb, LANES), F32)
    for k in range(TOP_K):
        member = member + jnp.where(onehot[k], 1.0, 0.0)
    ri = lax.broadcasted_iota(jnp.int32, (tb, tb), 0)
    rj = lax.broadcasted_iota(jnp.int32, (tb, tb), 1)
    before = _bf(jnp.where(rj < ri, 1.0, 0.0))
    slot = _dot(before, _bf(member)) + carry_ref[...] + off_ref[...]
    dest = jnp.zeros((tb, LANES), F32)
    for k in range(TOP_K):
        dest = jnp.where(lane == k, jnp.sum(jnp.where(onehot[k], slot, 0.0), axis=-1, keepdims=True), dest)
    dest_ref[...] = dest[:, 0:ROUTE_COLS].astype(jnp.int32) * ROW_TILE
    carry_ref[...] += jnp.sum(member, axis=0, keepdims=True)


def _slots(eidx, off, tb):
    n = eidx.shape[0]
    return pl.pallas_call(
        _slot_kernel,
        out_shape=jax.ShapeDtypeStruct((n, ROUTE_COLS), jnp.int32),
        grid=(n // tb,),
        in_specs=[pl.BlockSpec((tb, ROUTE_COLS), lambda i: (i, 0)),
                  pl.BlockSpec((1, LANES), lambda i: (0, 0))],
        out_specs=pl.BlockSpec((tb, ROUTE_COLS), lambda i: (i, 0)),
        scratch_shapes=[pltpu.VMEM((1, LANES), F32)],
        compiler_params=_params(("arbitrary",)),
        name="route_slots",
    )(eidx, off)


DMA_GROUP = 4


def _row_copy(src_ref, src_row, dst_ref, dst_row, sem):
    return pltpu.make_async_copy(src_ref.at[pl.ds(pl.multiple_of(src_row, ROW_TILE), ROW_TILE)],
                                 dst_ref.at[pl.ds(pl.multiple_of(dst_row, ROW_TILE), ROW_TILE)], sem)


def _scatter_kernel(dest_ref, ht_ref, xs_in_hbm, xs_hbm, sem, *, tb):
    del xs_in_hbm

    def start(g, carry):
        r0 = g * DMA_GROUP
        slots = [[dest_ref[(r0 + u) * ROUTE_COLS + k] for k in range(TOP_K)] for u in range(DMA_GROUP)]
        for u in range(DMA_GROUP):
            for k in range(TOP_K):
                _row_copy(ht_ref, (r0 + u) * ROW_TILE, xs_hbm, slots[u][k], sem).start()
        return carry

    def wait(r, carry):
        for k in range(TOP_K):
            _row_copy(ht_ref, 0, xs_hbm, 0, sem).wait()
        return carry

    lax.fori_loop(0, tb // DMA_GROUP, start, 0)
    lax.fori_loop(0, tb, wait, 0)


def _scatter_rows(ht, dest_flat, n_slots, tb):
    n = ht.shape[0] // ROW_TILE
    return pl.pallas_call(
        functools.partial(_scatter_kernel, tb=tb),
        out_shape=jax.ShapeDtypeStruct((n_slots * ROW_TILE, LANES), F32),
        grid=(n // tb,),
        in_specs=[pl.BlockSpec((tb * ROUTE_COLS,), lambda i: (i,), memory_space=pltpu.SMEM),
                  pl.BlockSpec((tb * ROW_TILE, LANES), lambda i: (i, 0)),
                  pl.BlockSpec(memory_space=pl.ANY)],
        out_specs=pl.BlockSpec(memory_space=pl.ANY),
        scratch_shapes=[pltpu.SemaphoreType.DMA],
        input_output_aliases={2: 0},
        compiler_params=_params(("arbitrary",)),
        name="route_scatter",
    )(dest_flat, ht, jnp.zeros((n_slots * ROW_TILE, LANES), F32))


def _experts_kernel(te_ref, nt_ref, xs_ref, wg_ref, wu_ref, wd_ref, ys_ref):
    del te_ref
    live = pl.program_id(0) < nt_ref[0]

    @pl.when(live)
    def _():
        tm = xs_ref.shape[0] // ROW_TILE
        xb = _bf(_from_row_tiles(xs_ref, tm))
        act = _silu(_dot(xb, _bf(wg_ref[0]))) * _dot(xb, _bf(wu_ref[0]))
        _to_row_tiles(ys_ref, _dot(_bf(act), _bf(wd_ref[0])))

    @pl.when(jnp.logical_not(live))
    def _():
        ys_ref[...] = jnp.zeros(ys_ref.shape, F32)


def _experts(xs, tile_expert, n_tiles, w_gate, w_up, w_down, tm):
    n_slots = xs.shape[0] // ROW_TILE
    return pl.pallas_call(
        _experts_kernel,
        out_shape=jax.ShapeDtypeStruct((n_slots * ROW_TILE, LANES), F32),
        grid_spec=pltpu.PrefetchScalarGridSpec(
            num_scalar_prefetch=2,
            grid=(n_slots // tm,),
            in_specs=[pl.BlockSpec((tm * ROW_TILE, LANES), lambda g, te, nt: (g, 0)),
                      pl.BlockSpec((1, D_MODEL, D_EXPERT), lambda g, te, nt: (te[g], 0, 0)),
                      pl.BlockSpec((1, D_MODEL, D_EXPERT), lambda g, te, nt: (te[g], 0, 0)),
                      pl.BlockSpec((1, D_EXPERT, D_MODEL), lambda g, te, nt: (te[g], 0, 0))],
            out_specs=pl.BlockSpec((tm * ROW_TILE, LANES), lambda g, te, nt: (g, 0))),
        compiler_params=_params(("arbitrary",)),
        name="routed_experts",
    )(tile_expert, n_tiles, xs, w_gate, w_up, w_down)


def _combine_kernel(dest_ref, dnext_ref, wsel_ref, h_ref, sg_ref, su_ref, sd_ref, g_ref, b_ref, ys_hbm,
                    out_ref, yg_ref, mix_ref, sem, *, tc):
    i = pl.program_id(0)
    last = pl.num_programs(0) - 1

    def fetch(d_ref, buf, wait):
        def body(g, carry):
            r0 = g * DMA_GROUP
            slots = [[0 if wait else d_ref[(r0 + u) * ROUTE_COLS + k] for k in range(TOP_K)]
                     for u in range(DMA_GROUP)]
            for u in range(DMA_GROUP):
                for k in range(TOP_K):
                    cp = _row_copy(ys_hbm, slots[u][k], yg_ref.at[buf, k], (r0 + u) * ROW_TILE, sem.at[buf])
                    cp.wait() if wait else cp.start()
            return carry
        lax.fori_loop(0, tc // DMA_GROUP, body, 0)

    cur = i % 2

    @pl.when(i == 0)
    def _():
        fetch(dest_ref, 0, False)

    @pl.when(jnp.logical_and(i < last, cur == 0))
    def _():
        fetch(dnext_ref, 1, False)

    @pl.when(jnp.logical_and(i < last, cur == 1))
    def _():
        fetch(dnext_ref, 0, False)

    h = h_ref[...]
    hb = _bf(h)
    shared = _dot(_bf(_silu(_dot(hb, sg_ref[...])) * _dot(hb, su_ref[...])), sd_ref[...])

    def finish(buf):
        fetch(dest_ref, buf, True)

        def mix(r, carry):
            row = pl.multiple_of(r * ROW_TILE, ROW_TILE)
            t = wsel_ref[r * ROUTE_COLS] * yg_ref[buf, 0, pl.ds(row, ROW_TILE), :]
            for k in range(1, TOP_K):
                t = t + wsel_ref[r * ROUTE_COLS + k] * yg_ref[buf, k, pl.ds(row, ROW_TILE), :]
            mix_ref[pl.ds(row, ROW_TILE), :] = t
            return carry

        lax.fori_loop(0, tc, mix, 0, unroll=4)
        acc = ALPHA * h + shared + _from_row_tiles(mix_ref, tc)
        out_ref[...] = _layer_norm(acc, g_ref[...], b_ref[...])

    @pl.when(cur == 0)
    def _():
        finish(0)

    @pl.when(cur == 1)
    def _():
        finish(1)


def _combine(h2d, ys, dest_flat, wsel_flat, sg_bf, su_bf, sd_bf, ln_g, ln_b, tc):
    n = h2d.shape[0]
    nblk = n // tc
    const = lambda shape: pl.BlockSpec(shape, lambda i: (0,) * len(shape))
    idx = lambda f: pl.BlockSpec((tc * ROUTE_COLS,), f, memory_space=pltpu.SMEM)
    return pl.pallas_call(
        functools.partial(_combine_kernel, tc=tc),
        out_shape=jax.ShapeDtypeStruct((n, D_MODEL), F32),
        grid=(nblk,),
        in_specs=[idx(lambda i: (i,)), idx(lambda i: (jnp.minimum(i + 1, nblk - 1),)), idx(lambda i: (i,)),
                  pl.BlockSpec((tc, D_MODEL), lambda i: (i, 0)),
                  const((D_MODEL, D_EXPERT)), const((D_MODEL, D_EXPERT)), const((D_EXPERT, D_MODEL)),
                  const((1, D_MODEL)), const((1, D_MODEL)),
                  pl.BlockSpec(memory_space=pl.ANY)],
        out_specs=pl.BlockSpec((tc, D_MODEL), lambda i: (i, 0)),
        scratch_shapes=[pltpu.VMEM((2, TOP_K, tc * ROW_TILE, LANES), F32),
                        pltpu.VMEM((tc * ROW_TILE, LANES), F32), pltpu.SemaphoreType.DMA((2,))],
        compiler_params=_params(("arbitrary",)),
        name="route_combine",
    )(dest_flat, dest_flat, wsel_flat, h2d, sg_bf, su_bf, sd_bf, ln_g, ln_b, ys)


def _moe(h2d, ht, w, *, tm):
    n = h2d.shape[0]
    n_tiles_max = (n * TOP_K) // tm + N_EXPERTS
    eidx, wsel, counts = _router(h2d, w["w_router"], w["e_bias"], _row_tile(n, 512))
    cnt = counts[0, :N_EXPERTS].astype(jnp.int32)
    tiles = (cnt + tm - 1) // tm
    tile_end = jnp.cumsum(tiles)
    off = ((tile_end - tiles) * tm).astype(F32)
    tile_expert = jnp.minimum(
        jnp.sum(tile_end[None, :] <= jnp.arange(n_tiles_max, dtype=jnp.int32)[:, None], axis=1),
        N_EXPERTS - 1).astype(jnp.int32)
    dest = _slots(eidx, jnp.pad(off, (0, LANES - N_EXPERTS))[None, :], _row_tile(n, 256))
    dest_flat = dest.reshape(n * ROUTE_COLS)
    xs = _scatter_rows(ht, dest_flat, n_tiles_max * tm, _row_tile(n, 128))
    ys = _experts(xs, tile_expert, tile_end[N_EXPERTS - 1:], w["w_gate"], w["w_up"], w["w_down"], tm)
    return _combine(h2d, ys, dest_flat, wsel.reshape(n * ROUTE_COLS), w["ws_gate"], w["ws_up"], w["ws_down"],
                    w["ln2_g"], w["ln2_b"], _row_tile(n, 128))


def _row_tile(n, cap):
    t = cap
    while n % t:
        t //= 2
    return t


def _trunk(x, s_hg, s_rw, shift, w, *, hg_cfg, rw_tb, moe_tm):
    B, T, _ = x.shape
    n = B * T
    x2d = x.reshape(n, D_MODEL)
    p_hg, p_rw = _proj(x2d, w["w_in"], _row_tile(n, 512))
    p_hg = p_hg.reshape(B, T, HG_COLS)
    p_rw = p_rw.reshape(B, T, RW_COLS)
    hg_out, hg_state = _hgrn(p_hg, s_hg, w["hg_lb"], w["hg_norm_g"], **hg_cfg)
    rw_out, rw_state = _rwkv(p_rw, shift, s_rw, w["rw_mu"], w["rw_w0"], w["rw_wa"], w["rw_a0"],
                             w["rw_g2"], w["rw_k_k"], w["rw_k_a"], w["rw_r_k"], w["rw_gn_g"],
                             w["rw_gn_b"], tb=rw_tb)
    h, ht = _outproj(x2d, hg_out.reshape(n, HG_WIDTH), rw_out.reshape(n, RW_WIDTH), w["w_out"],
                     w["ln1_g"], w["ln1_b"], _row_tile(n, 512))
    y = _moe(h, ht, w, tm=moe_tm)
    return y.reshape(B, T, D_MODEL), hg_state, rw_state, p_rw[:, T - 1, :]


def kernel(x_prompt, x_sample, state_hgrn, state_rwkv, state_shift, w_in, w_out, hg_lb, hg_norm_g, rw_mu, rw_w0, rw_w2, rw_a0, rw_a2, rw_g2, rw_k_k, rw_k_a, rw_r_k, rw_gn_g, rw_gn_b, ln1_g, ln1_b, ln2_g, ln2_b, w_router, e_bias, w_gate, w_up, w_down, ws_gate, ws_up, ws_down):
    l = 0
    zeros_lora = jnp.zeros((RW_DH, RW_WIDTH), F32)
    w = {
        "w_in": _bf(w_in[l]), "w_out": _bf(w_out[l]),
        "hg_lb": hg_lb, "hg_norm_g": hg_norm_g[l][None, :],
        "rw_mu": rw_mu[l][None, :], "rw_w0": rw_w0[l][None, :], "rw_a0": rw_a0[l][None, :],
        "rw_wa": _bf(jnp.concatenate([jnp.concatenate([rw_w2[l], zeros_lora], axis=1),
                                      jnp.concatenate([zeros_lora, rw_a2[l]], axis=1)], axis=0)),
        "rw_g2": _bf(rw_g2[l]),
        "rw_k_k": rw_k_k[l][None, :], "rw_k_a": rw_k_a[l][None, :],
        "rw_r_k": rw_r_k[l].reshape(1, RW_WIDTH),
        "rw_gn_g": rw_gn_g[l][None, :], "rw_gn_b": rw_gn_b[l][None, :],
        "ln1_g": ln1_g[l][None, :], "ln1_b": ln1_b[l][None, :],
        "ln2_g": ln2_g[l][None, :], "ln2_b": ln2_b[l][None, :],
        "w_router": jnp.pad(w_router[l], ((0, 0), (0, LANES - N_EXPERTS))),
        "e_bias": jnp.pad(e_bias[l], (0, LANES - N_EXPERTS))[None, :],
        "w_gate": w_gate[l], "w_up": w_up[l], "w_down": w_down[l],
        "ws_gate": _bf(ws_gate[l]), "ws_up": _bf(ws_up[l]), "ws_down": _bf(ws_down[l]),
    }
    Bp = x_prompt.shape[0]
    zero_hg = jnp.zeros((Bp, HG_HEADS, HG_DK, HG_DK), F32)
    zero_rw = jnp.zeros((Bp, RW_HEADS, RW_DH, RW_DH), F32)
    zero_sh = jnp.zeros((Bp, RW_COLS), F32)
    yp, hg_p, rw_p, sh_p = _trunk(x_prompt, zero_hg, zero_rw, zero_sh, w,
                                  hg_cfg=dict(Bb=1, Tb=128, C=64, R=16), rw_tb=64, moe_tm=256)
    ys, hg_s, rw_s, sh_s = _trunk(x_sample, state_hgrn[l], state_rwkv[l], state_shift[l], w,
                                  hg_cfg=dict(Bb=8, Tb=8, C=8, R=8), rw_tb=8, moe_tm=128)
    return (yp, ys, hg_p[None], rw_p[None], sh_p[None], hg_s[None], rw_s[None], sh_s[None])
```

```python
import functools

import jax
import jax.numpy as jnp
from jax import lax
from jax.experimental import pallas as pl
from jax.experimental.pallas import tpu as pltpu

F32 = jnp.float32
BF16 = jnp.bfloat16

D_MODEL = 1024
HG_HEADS = 4
HG_DK = 128
HG_WIDTH = 512
HG_COLS = 2048
RW_HEADS = 8
RW_DH = 64
RW_WIDTH = 512
RW_COLS = 1792
N_EXPERTS = 64
TOP_K = 6
N_GROUPS = 8
TOPK_GROUPS = 4
GROUP_SIZE = N_EXPERTS // N_GROUPS
D_EXPERT = 256
ROUTED_SCALE = 2.5
DEPTH = 1
ALPHA = (2.0 * DEPTH) ** 0.25
LN_EPS = 1e-5
HEAD_NORM_EPS = 1e-6
RW_GN_EPS = 64e-5

LANES = 128
VMEM_LIMIT = 56 * 1024 * 1024


def _bf(x):
    return x.astype(BF16)


def _dot(a, b):
    return jnp.dot(a, b, preferred_element_type=F32)


def _dot_nt(a, b):
    return lax.dot_general(a, b, (((1,), (1,)), ((), ())), preferred_element_type=F32)


def _split2(x):
    hi = _bf(x)
    lo = _bf(x - hi.astype(F32))
    return hi, lo


def _split3(x):
    hi = _bf(x)
    r1 = x - hi.astype(F32)
    mid = _bf(r1)
    lo = _bf(r1 - mid.astype(F32))
    return hi, mid, lo


def _sigmoid(x):
    return 1.0 / (1.0 + jnp.exp(-x))


def _silu(x):
    return x * _sigmoid(x)


def _params(sem):
    return pltpu.CompilerParams(dimension_semantics=sem, vmem_limit_bytes=VMEM_LIMIT)


def _proj_kernel(x_ref, w_ref, hg_ref, rw_ref):
    xb = _bf(x_ref[...])
    step = 256
    for c0 in range(0, HG_COLS, step):
        hg_ref[:, c0:c0 + step] = _dot(xb, w_ref[:, c0:c0 + step])
    for c0 in range(0, RW_COLS, step):
        rw_ref[:, c0:c0 + step] = _dot(xb, w_ref[:, HG_COLS + c0:HG_COLS + c0 + step])


def _proj(x2d, w_in_bf, tm):
    n = x2d.shape[0]
    return pl.pallas_call(
        _proj_kernel,
        out_shape=(jax.ShapeDtypeStruct((n, HG_COLS), F32), jax.ShapeDtypeStruct((n, RW_COLS), F32)),
        grid=(n // tm,),
        in_specs=[pl.BlockSpec((tm, D_MODEL), lambda i: (i, 0)),
                  pl.BlockSpec((D_MODEL, HG_COLS + RW_COLS), lambda i: (0, 0))],
        out_specs=(pl.BlockSpec((tm, HG_COLS), lambda i: (i, 0)),
                   pl.BlockSpec((tm, RW_COLS), lambda i: (i, 0))),
        compiler_params=_params(("parallel",)),
        name="in_proj",
    )(x2d, w_in_bf)


def _hgrn_kernel(p_ref, s0_ref, lbraw_ref, ng_ref, out_ref, sfin_ref, st_ref, *, Bb, Tb, C, R):
    ti = pl.program_id(1)
    nt = pl.num_programs(1)
    nb = C // R

    @pl.when(ti == 0)
    def _():
        for b in range(Bb):
            for h in range(HG_HEADS):
                st_ref[b, h] = s0_ref[b, h].T

    lbr = lbraw_ref[...]
    lbe = jnp.exp(lbr - jnp.max(lbr, axis=0, keepdims=True))
    lb = lbe[0:1, :] / jnp.sum(lbe, axis=0, keepdims=True)
    ng = ng_ref[...]

    if Tb > 8:
        rows = lax.broadcasted_iota(jnp.int32, (Tb, Tb), 0)
        cols = lax.broadcasted_iota(jnp.int32, (Tb, Tb), 1)
        tril = _bf(jnp.where(((rows // C) == (cols // C)) & (cols <= rows), 1.0, 0.0))
    tidx = lax.broadcasted_iota(jnp.int32, (R, 1), 0)
    cidx = lax.broadcasted_iota(jnp.int32, (C, 1), 0)

    for b in range(Bb):
        qs = _silu(p_ref[b, :, 0:512])
        f = lb + (1.0 - lb) * _sigmoid(p_ref[b, :, 512:1024])
        lf = jnp.log(f)
        kd = 1.0 - f
        v = p_ref[b, :, 1024:1536]
        if Tb == 8:
            r8 = lax.broadcasted_iota(jnp.int32, (8, 1), 0)
            bcum = jnp.concatenate(
                [jnp.sum(jnp.where(r8 <= t, lf, 0.0), axis=0, keepdims=True) for t in range(8)], axis=0)
        else:
            hi, mid, lo = _split3(lf)
            bcum = _dot(tril, hi) + _dot(tril, mid) + _dot(tril, lo)
        o_heads = []
        for h in range(HG_HEADS):
            sl = slice(h * HG_DK, (h + 1) * HG_DK)
            st = st_ref[b, h]
            o_chunks = []
            for c in range(Tb // C):
                r0 = c * C
                bc = bcum[r0:r0 + C, sl]
                qc = qs[r0:r0 + C, sl]
                kc = kd[r0:r0 + C, sl]
                vc = v[r0:r0 + C, sl]
                b_last = bc[C - 1:C, :]
                o_inter = _dot_nt(_bf(qc * jnp.exp(bc)), _bf(st))
                o_blocks = []
                for i in range(nb):
                    rs = slice(i * R, (i + 1) * R)
                    bq = bc[rs]
                    qq = qc[rs]
                    acc = jnp.zeros((R, HG_DK), F32)
                    if i > 0:
                        anchor = bc[i * R - 1:i * R, :]
                        early = cidx < i * R
                        qa = qq * jnp.exp(bq - anchor)
                        ka = jnp.where(early, kc * jnp.exp(jnp.where(early, anchor - bc, 0.0)), 0.0)
                        att = _dot_nt(_bf(qa), _bf(ka))
                        acc = acc + _dot(_bf(att), _bf(vc))
                    for s in range(R):
                        sa = i * R + s
                        msk = tidx >= s
                        e = jnp.exp(jnp.where(msk, bq - bc[sa:sa + 1, :], 0.0))
                        term = jnp.where(msk, qq * kc[sa:sa + 1, :] * e, 0.0)
                        acc = acc + jnp.sum(term, axis=-1, keepdims=True) * vc[sa:sa + 1, :]
                    o_blocks.append(acc)
                o_intra = o_blocks[0] if nb == 1 else jnp.concatenate(o_blocks, axis=0)
                o_chunks.append(o_inter + o_intra)
                k_out = kc * jnp.exp(b_last - bc)
                st = st * jnp.exp(b_last) + _dot(_bf(vc.T), _bf(k_out))
            st_ref[b, h] = st
            o_h = o_chunks[0] if len(o_chunks) == 1 else jnp.concatenate(o_chunks, axis=0)
            o_h = o_h * lax.rsqrt(jnp.mean(o_h * o_h, axis=-1, keepdims=True) + HEAD_NORM_EPS) * ng[:, sl]
            o_heads.append(o_h)
        out_ref[b] = jnp.concatenate(o_heads, axis=-1) * _silu(p_ref[b, :, 1536:2048])

    @pl.when(ti == nt - 1)
    def _():
        for b in range(Bb):
            for h in range(HG_HEADS):
                sfin_ref[b, h] = st_ref[b, h].T


def _hgrn(p_hg, s0, lb_raw, norm_g, *, Bb, Tb, C, R):
    B, T, _ = p_hg.shape
    kern = functools.partial(_hgrn_kernel, Bb=Bb, Tb=Tb, C=C, R=R)
    return pl.pallas_call(
        kern,
        out_shape=(jax.ShapeDtypeStruct((B, T, HG_WIDTH), F32),
                   jax.ShapeDtypeStruct((B, HG_HEADS, HG_DK, HG_DK), F32)),
        grid=(B // Bb, T // Tb),
        in_specs=[pl.BlockSpec((Bb, Tb, HG_COLS), lambda i, j: (i, j, 0)),
                  pl.BlockSpec((Bb, HG_HEADS, HG_DK, HG_DK), lambda i, j: (i, 0, 0, 0)),
                  pl.BlockSpec((DEPTH + 1, HG_WIDTH), lambda i, j: (0, 0)),
                  pl.BlockSpec((1, HG_WIDTH), lambda i, j: (0, 0))],
        out_specs=(pl.BlockSpec((Bb, Tb, HG_WIDTH), lambda i, j: (i, j, 0)),
                   pl.BlockSpec((Bb, HG_HEADS, HG_DK, HG_DK), lambda i, j: (i, 0, 0, 0))),
        scratch_shapes=[pltpu.VMEM((Bb, HG_HEADS, HG_DK, HG_DK), F32)],
        compiler_params=_params(("parallel", "arbitrary")),
        name="hgrn2_mixer",
    )(p_hg, s0, lb_raw, norm_g)


RW_PAIRS = RW_HEADS // 2
EXP_NEG_HALF = 0.6065306597126334
RW_SEQ = 8
RW_AHEAD = 6


def _block_ones():
    i = lax.broadcasted_iota(jnp.int32, (LANES, LANES), 0)
    j = lax.broadcasted_iota(jnp.int32, (LANES, LANES), 1)
    return _bf(jnp.where((i // RW_DH) == (j // RW_DH), 1.0, 0.0))


def _seg_sum(x, bd):
    outs = []
    for c in range(RW_WIDTH // LANES):
        hi, lo = _split2(x[:, c * LANES:(c + 1) * LANES])
        outs.append(_dot(hi, bd) + _dot(lo, bd))
    return jnp.concatenate(outs, axis=-1)


def _rwkv_kernel(p_ref, shift_ref, s0_ref, mu_ref, w0_ref, wa_ref, a0_ref, g2_ref, kk_ref, ka_ref,
                 rk_ref, gng_ref, gnb_ref, out_ref, sfin_ref,
                 s_ref, prev_ref, r_s, w_s, k_s, v_s, kk_s, kka_s, g_s, vt_s, o_s, *, tb):
    ti = pl.program_id(1)
    nt = pl.num_programs(1)
    pairs = range(RW_PAIRS)
    sub = 8

    @pl.when(ti == 0)
    def _():
        prev_ref[...] = shift_ref[...]
        for b in range(RW_SEQ):
            for p in pairs:
                s_ref[b, p * RW_DH:(p + 1) * RW_DH, :] = jnp.concatenate(
                    [s0_ref[b, 2 * p], s0_ref[b, 2 * p + 1]], axis=-1)

    bd = _block_ones()
    lane = lax.broadcasted_iota(jnp.int32, (1, LANES), 1)
    rid = lax.broadcasted_iota(jnp.int32, (tb, 1), 0)
    mu = mu_ref[...]

    for b in range(RW_SEQ):
        p = p_ref[b]
        prev = jnp.where(rid == 0, prev_ref[b:b + 1, :], pltpu.roll(p, 1, axis=0))
        prev_ref[b:b + 1, :] = p[tb - 1:tb, :]
        ps = p + mu * (prev - p)
        r = ps[:, 0:512]
        k = ps[:, 512:1024]
        v = ps[:, 1024:1536]
        zz = ps[:, 1536:1664]
        zg = ps[:, 1664:1792]
        lora = _dot(_bf(jnp.where(lane < 64, jnp.tanh(zz), zz)), wa_ref[...])
        exp_w = EXP_NEG_HALF / (1.0 + jnp.exp(-(w0_ref[...] + lora[:, 0:512])))
        a = _sigmoid(a0_ref[...] + lora[:, 512:1024])
        kk = k * kk_ref[...]
        kk = kk / jnp.maximum(jnp.sqrt(_seg_sum(kk * kk, bd)), 1e-12)
        r_s[b] = r
        w_s[b] = jnp.exp(-exp_w)
        k_s[b] = k * (1.0 + (a - 1.0) * ka_ref[...])
        v_s[b] = v
        kk_s[b] = kk
        kka_s[b] = kk * a
        g_s[b] = _dot(_bf(_sigmoid(zg)), g2_ref[...])
        for q in pairs:
            vp = v[:, q * LANES:(q + 1) * LANES]
            if tb < RW_DH:
                vp = jnp.concatenate([vp, jnp.zeros((RW_DH - tb, LANES), F32)], axis=0)
            vpt = vp.T
            vt_s[b, q * RW_DH:(q + 1) * RW_DH, :] = _bf(
                jnp.concatenate([vpt[0:RW_DH, :], vpt[RW_DH:LANES, :]], axis=1))

    ci = lax.broadcasted_iota(jnp.int32, (LANES, LANES), 0)
    cj = lax.broadcasted_iota(jnp.int32, (LANES, LANES), 1)
    same_head = (ci // RW_DH) == (cj // RW_DH)
    si = lax.broadcasted_iota(jnp.int32, (16, 2 * LANES), 0)
    sj = lax.broadcasted_iota(jnp.int32, (16, 2 * LANES), 1)
    head_rows = _bf(jnp.where(si == sj // RW_DH, 1.0, 0.0))

    def rows(ref, b, t0, j):
        return [ref[b, pl.ds(t0, sub), pl.ds(q * LANES, LANES)][j:j + 1, :] for q in pairs]

    def scale(x, rws):
        return jnp.concatenate([x[q * RW_DH:(q + 1) * RW_DH] * rws[q] for q in pairs], axis=0)

    def side_by_side(x):
        return jnp.concatenate(
            [jnp.concatenate([x[0:RW_DH], x[RW_DH:2 * RW_DH]], axis=1),
             jnp.concatenate([x[2 * RW_DH:3 * RW_DH], x[3 * RW_DH:4 * RW_DH]], axis=1)], axis=0)

    def steps(tq, carry):
        t0 = pl.multiple_of(tq * sub, sub)
        for j in range(sub):
            tl = tq * sub + j
            pick = _bf(jnp.where(same_head & ((ci % RW_DH) == tl), 1.0, 0.0))
            def reductions(b):
                return (_dot(_bf(scale(s_ref[b], rows(kk_s, b, t0, j))), bd), _dot(vt_s[b], pick))

            ahead = [reductions(b) for b in range(RW_AHEAD)]
            for b in range(RW_SEQ):
                sa, vb = ahead.pop(0)
                if b + RW_AHEAD < RW_SEQ:
                    ahead.append(reductions(b + RW_AHEAD))
                s = (scale(s_ref[b], rows(w_s, b, t0, j)) - scale(sa, rows(kka_s, b, t0, j))
                     + scale(vb, rows(k_s, b, t0, j)))
                s_ref[b] = s
                o4 = _dot_nt(head_rows, _bf(side_by_side(scale(s, rows(r_s, b, t0, j)))))
                for h in range(RW_PAIRS):
                    o_s[h, tq, j:j + 1, b * LANES:(b + 1) * LANES] = o4[h:h + 1, :]
        return carry

    lax.fori_loop(0, tb // sub, steps, 0)

    for b in range(RW_SEQ):
        lanes = slice(b * LANES, (b + 1) * LANES)
        nq = tb // sub
        oh = [jnp.concatenate([o_s[h, q, :, lanes] for q in range(nq)], axis=0) if nq > 1
              else o_s[h, 0, :, lanes] for h in range(RW_PAIRS)]
        o = jnp.concatenate([oh[h % RW_PAIRS][:, (h // RW_PAIRS) * RW_DH:(h // RW_PAIRS + 1) * RW_DH]
                             for h in range(RW_HEADS)], axis=-1)
        d = o - _seg_sum(o, bd) * (1.0 / RW_DH)
        var = _seg_sum(d * d, bd) * (1.0 / RW_DH)
        on = d * lax.rsqrt(var + RW_GN_EPS) * gng_ref[...] + gnb_ref[...]
        bonus = _seg_sum(r_s[b] * k_s[b] * rk_ref[...], bd) * v_s[b]
        out_ref[b] = (on + bonus) * g_s[b]

    @pl.when(ti == nt - 1)
    def _():
        for b in range(RW_SEQ):
            for p in pairs:
                s = s_ref[b, p * RW_DH:(p + 1) * RW_DH, :]
                sfin_ref[b, 2 * p] = s[:, 0:RW_DH]
                sfin_ref[b, 2 * p + 1] = s[:, RW_DH:LANES]


def _rwkv(p_rw, shift, s0, mu, w0, wa, a0, g2, k_k, k_a, r_k, gn_g, gn_b, *, tb):
    B, T, _ = p_rw.shape
    kern = functools.partial(_rwkv_kernel, tb=tb)
    vec = lambda n: pl.BlockSpec((1, n), lambda i, j: (0, 0))
    blk = pltpu.VMEM((RW_SEQ, tb, RW_WIDTH), F32)
    oblk = pltpu.VMEM((RW_PAIRS, tb // 8, 8, RW_SEQ * LANES), F32)
    return pl.pallas_call(
        kern,
        out_shape=(jax.ShapeDtypeStruct((B, T, RW_WIDTH), F32),
                   jax.ShapeDtypeStruct((B, RW_HEADS, RW_DH, RW_DH), F32)),
        grid=(B // RW_SEQ, T // tb),
        in_specs=[pl.BlockSpec((RW_SEQ, tb, RW_COLS), lambda i, j: (i, j, 0)),
                  pl.BlockSpec((RW_SEQ, RW_COLS), lambda i, j: (i, 0)),
                  pl.BlockSpec((RW_SEQ, RW_HEADS, RW_DH, RW_DH), lambda i, j: (i, 0, 0, 0)),
                  vec(RW_COLS), vec(RW_WIDTH),
                  pl.BlockSpec((LANES, 2 * RW_WIDTH), lambda i, j: (0, 0)),
                  vec(RW_WIDTH),
                  pl.BlockSpec((LANES, RW_WIDTH), lambda i, j: (0, 0)),
                  vec(RW_WIDTH), vec(RW_WIDTH), vec(RW_WIDTH), vec(RW_WIDTH), vec(RW_WIDTH)],
        out_specs=(pl.BlockSpec((RW_SEQ, tb, RW_WIDTH), lambda i, j: (i, j, 0)),
                   pl.BlockSpec((RW_SEQ, RW_HEADS, RW_DH, RW_DH), lambda i, j: (i, 0, 0, 0))),
        scratch_shapes=[pltpu.VMEM((RW_SEQ, RW_PAIRS * RW_DH, LANES), F32),
                        pltpu.VMEM((RW_SEQ, RW_COLS), F32),
                        blk, blk, blk, blk, blk, blk, blk,
                        pltpu.VMEM((RW_SEQ, RW_PAIRS * RW_DH, LANES), BF16),
                        oblk],
        compiler_params=_params(("parallel", "arbitrary")),
        name="rwkv7_mixer",
    )(p_rw, shift, s0, mu, w0, wa, a0, g2, k_k, k_a, r_k, gn_g, gn_b)


def _layer_norm(y, g, b):
    mu = jnp.mean(y, axis=-1, keepdims=True)
    d = y - mu
    var = jnp.mean(d * d, axis=-1, keepdims=True)
    return d * lax.rsqrt(var + LN_EPS) * g + b


ROW_TILE = 8


def _to_row_tiles(ref, x):
    rows = x.shape[0]
    for s in range(ROW_TILE):
        ref[pl.ds(s, rows, stride=ROW_TILE), :] = x[:, s * LANES:(s + 1) * LANES]


def _from_row_tiles(ref, rows):
    return jnp.concatenate([ref[pl.ds(s, rows, stride=ROW_TILE), :] for s in range(ROW_TILE)], axis=-1)


def _outproj_kernel(x_ref, hg_ref, rw_ref, w_ref, g_ref, b_ref, h_ref, ht_ref):
    mix = (_dot(_bf(hg_ref[...]), w_ref[0:HG_WIDTH, :])
           + _dot(_bf(rw_ref[...]), w_ref[HG_WIDTH:HG_WIDTH + RW_WIDTH, :]))
    h = _layer_norm(ALPHA * x_ref[...] + mix, g_ref[...], b_ref[...])
    h_ref[...] = h
    _to_row_tiles(ht_ref, h)


def _outproj(x2d, hg2d, rw2d, w_out_bf, ln_g, ln_b, tm):
    n = x2d.shape[0]
    rows = lambda w: pl.BlockSpec((tm, w), lambda i: (i, 0))
    return pl.pallas_call(
        _outproj_kernel,
        out_shape=(jax.ShapeDtypeStruct((n, D_MODEL), F32),
                   jax.ShapeDtypeStruct((n * ROW_TILE, LANES), F32)),
        grid=(n // tm,),
        in_specs=[rows(D_MODEL), rows(HG_WIDTH), rows(RW_WIDTH),
                  pl.BlockSpec((D_MODEL, D_MODEL), lambda i: (0, 0)),
                  pl.BlockSpec((1, D_MODEL), lambda i: (0, 0)),
                  pl.BlockSpec((1, D_MODEL), lambda i: (0, 0))],
        out_specs=(rows(D_MODEL), pl.BlockSpec((tm * ROW_TILE, LANES), lambda i: (i, 0))),
        compiler_params=_params(("parallel",)),
        name="out_proj_ln",
    )(x2d, hg2d, rw2d, w_out_bf, ln_g, ln_b)


ROUTE_COLS = 8


def _router_kernel(h_ref, w_ref, eb_ref, eidx_ref, wsel_ref, cnt_ref):
    tm = h_ref.shape[0]
    hh, hl = _split2(h_ref[...])
    wh, wl = _split2(w_ref[...])
    logits = _dot(hh, wh) + (_dot(hh, wl) + _dot(hl, wh))
    lane = lax.broadcasted_iota(jnp.int32, (tm, LANES), 1)
    valid = lane < N_EXPERTS
    neg = -jnp.inf
    scores = _sigmoid(logits)
    choice = jnp.where(valid, scores + eb_ref[...], neg)

    def partner(x, s):
        return jnp.where((lane & s) != 0, pltpu.roll(x, s, axis=1), pltpu.roll(x, LANES - s, axis=1))

    def group_max(x):
        for s in (1, 2, 4):
            x = jnp.maximum(x, partner(x, s))
        return x

    def group_min(x):
        for s in (1, 2, 4):
            x = jnp.minimum(x, partner(x, s))
        return x

    m1 = group_max(choice)
    first = group_min(jnp.where(choice == m1, lane, LANES))
    m2 = group_max(jnp.where(lane == first, neg, choice))
    gscore = m1 + m2
    gper = jnp.where(valid, gscore, pltpu.roll(gscore, N_EXPERTS, axis=1))
    gidx = (lane % N_EXPERTS) // GROUP_SIZE
    rank = jnp.zeros((tm, LANES), jnp.int32)
    for j in range(1, N_GROUPS):
        other = pltpu.roll(gper, GROUP_SIZE * j, axis=1)
        beats = (other > gper) | ((other == gper) & (gidx >= j))
        rank = rank + beats.astype(jnp.int32)
    cand = jnp.where(valid & (rank < TOPK_GROUPS), choice, neg)
    sel = jnp.zeros((tm, LANES), jnp.bool_)
    hits = []
    eidx = jnp.zeros((tm, LANES), jnp.int32)
    for k in range(TOP_K):
        m = jnp.max(cand, axis=-1, keepdims=True)
        idx = jnp.min(jnp.where(cand == m, lane, LANES), axis=-1, keepdims=True)
        hit = lane == idx
        hits.append(hit)
        eidx = jnp.where(lane == k, idx, eidx)
        sel = sel | hit
        cand = jnp.where(hit, neg, cand)
    wts = jnp.where(sel, scores, 0.0)
    gates = wts / jnp.sum(wts, axis=-1, keepdims=True) * ROUTED_SCALE
    wsel = jnp.zeros((tm, LANES), F32)
    for k in range(TOP_K):
        wsel = jnp.where(lane == k, jnp.sum(jnp.where(hits[k], gates, 0.0), axis=-1, keepdims=True), wsel)
    eidx_ref[...] = eidx[:, 0:ROUTE_COLS]
    wsel_ref[...] = wsel[:, 0:ROUTE_COLS]

    @pl.when(pl.program_id(0) == 0)
    def _():
        cnt_ref[...] = jnp.zeros((1, LANES), F32)

    cnt_ref[...] += jnp.sum(jnp.where(sel, 1.0, 0.0), axis=0, keepdims=True)


def _router(h2d, w_router_pad, e_bias_pad, tm):
    n = h2d.shape[0]
    return pl.pallas_call(
        _router_kernel,
        out_shape=(jax.ShapeDtypeStruct((n, ROUTE_COLS), jnp.int32),
                   jax.ShapeDtypeStruct((n, ROUTE_COLS), F32),
                   jax.ShapeDtypeStruct((1, LANES), F32)),
        grid=(n // tm,),
        in_specs=[pl.BlockSpec((tm, D_MODEL), lambda i: (i, 0)),
                  pl.BlockSpec((D_MODEL, LANES), lambda i: (0, 0)),
                  pl.BlockSpec((1, LANES), lambda i: (0, 0))],
        out_specs=(pl.BlockSpec((tm, ROUTE_COLS), lambda i: (i, 0)),
                   pl.BlockSpec((tm, ROUTE_COLS), lambda i: (i, 0)),
                   pl.BlockSpec((1, LANES), lambda i: (0, 0))),
        compiler_params=_params(("arbitrary",)),
        name="router",
    )(h2d, w_router_pad, e_bias_pad)


def _slot_kernel(eidx_ref, off_ref, dest_ref, carry_ref):
    tb = eidx_ref.shape[0]

    @pl.when(pl.program_id(0) == 0)
    def _():
        carry_ref[...] = jnp.zeros((1, LANES), F32)

    lane = lax.broadcasted_iota(jnp.int32, (tb, LANES), 1)
    eidx = eidx_ref[...]
    onehot = [lane == eidx[:, k:k + 1] for k in range(TOP_K)]
    member = jnp.zeros((tb, LANES), F32)
    for k in range(TOP_K):
        member = member + jnp.where(onehot[k], 1.0, 0.0)
    ri = lax.broadcasted_iota(jnp.int32, (tb, tb), 0)
    rj = lax.broadcasted_iota(jnp.int32, (tb, tb), 1)
    before = _bf(jnp.where(rj < ri, 1.0, 0.0))
    slot = _dot(before, _bf(member)) + carry_ref[...] + off_ref[...]
    dest = jnp.zeros((tb, LANES), F32)
    for k in range(TOP_K):
        dest = jnp.where(lane == k, jnp.sum(jnp.where(onehot[k], slot, 0.0), axis=-1, keepdims=True), dest)
    dest_ref[...] = dest[:, 0:ROUTE_COLS].astype(jnp.int32) * ROW_TILE
    carry_ref[...] += jnp.sum(member, axis=0, keepdims=True)


def _slots(eidx, off, tb):
    n = eidx.shape[0]
    return pl.pallas_call(
        _slot_kernel,
        out_shape=jax.ShapeDtypeStruct((n, ROUTE_COLS), jnp.int32),
        grid=(n // tb,),
        in_specs=[pl.BlockSpec((tb, ROUTE_COLS), lambda i: (i, 0)),
                  pl.BlockSpec((1, LANES), lambda i: (0, 0))],
        out_specs=pl.BlockSpec((tb, ROUTE_COLS), lambda i: (i, 0)),
        scratch_shapes=[pltpu.VMEM((1, LANES), F32)],
        compiler_params=_params(("arbitrary",)),
        name="route_slots",
    )(eidx, off)


DMA_GROUP = 4


def _row_copy(src_ref, src_row, dst_ref, dst_row, sem):
    return pltpu.make_async_copy(src_ref.at[pl.ds(pl.multiple_of(src_row, ROW_TILE), ROW_TILE)],
                                 dst_ref.at[pl.ds(pl.multiple_of(dst_row, ROW_TILE), ROW_TILE)], sem)


def _scatter_kernel(dest_ref, ht_ref, xs_in_hbm, xs_hbm, sem, *, tb):
    del xs_in_hbm

    def start(g, carry):
        r0 = g * DMA_GROUP
        slots = [[dest_ref[(r0 + u) * ROUTE_COLS + k] for k in range(TOP_K)] for u in range(DMA_GROUP)]
        for u in range(DMA_GROUP):
            for k in range(TOP_K):
                _row_copy(ht_ref, (r0 + u) * ROW_TILE, xs_hbm, slots[u][k], sem).start(priority=k % 2)
        return carry

    def wait(r, carry):
        for k in range(TOP_K):
            _row_copy(ht_ref, 0, xs_hbm, 0, sem).wait()
        return carry

    lax.fori_loop(0, tb // DMA_GROUP, start, 0)
    lax.fori_loop(0, tb, wait, 0)


def _scatter_rows(ht, dest_flat, n_slots, tb):
    n = ht.shape[0] // ROW_TILE
    return pl.pallas_call(
        functools.partial(_scatter_kernel, tb=tb),
        out_shape=jax.ShapeDtypeStruct((n_slots * ROW_TILE, LANES), F32),
        grid=(n // tb,),
        in_specs=[pl.BlockSpec((tb * ROUTE_COLS,), lambda i: (i,), memory_space=pltpu.SMEM),
                  pl.BlockSpec((tb * ROW_TILE, LANES), lambda i: (i, 0)),
                  pl.BlockSpec(memory_space=pl.ANY)],
        out_specs=pl.BlockSpec(memory_space=pl.ANY),
        scratch_shapes=[pltpu.SemaphoreType.DMA],
        input_output_aliases={2: 0},
        compiler_params=_params(("arbitrary",)),
        name="route_scatter",
    )(dest_flat, ht, jnp.zeros((n_slots * ROW_TILE, LANES), F32))


def _experts_kernel(te_ref, nt_ref, xs_ref, wg_ref, wu_ref, wd_ref, ys_ref, wgu_bf, wd_bf):
    g = pl.program_id(0)
    live = g < nt_ref[0]

    @pl.when(jnp.logical_or(g == 0, te_ref[g] != te_ref[jnp.maximum(g - 1, 0)]))
    def _():
        wgu_bf[:, 0:D_EXPERT] = _bf(wg_ref[0])
        wgu_bf[:, D_EXPERT:2 * D_EXPERT] = _bf(wu_ref[0])
        wd_bf[...] = _bf(wd_ref[0])

    @pl.when(live)
    def _():
        tm = xs_ref.shape[0] // ROW_TILE
        xb = _bf(_from_row_tiles(xs_ref, tm))
        gu = _dot(xb, wgu_bf[...])
        act = _silu(gu[:, 0:D_EXPERT]) * gu[:, D_EXPERT:2 * D_EXPERT]
        _to_row_tiles(ys_ref, _dot(_bf(act), wd_bf[...]))

    @pl.when(jnp.logical_not(live))
    def _():
        ys_ref[...] = jnp.zeros(ys_ref.shape, F32)


def _experts(xs, tile_expert, n_tiles, w_gate, w_up, w_down, tm):
    n_slots = xs.shape[0] // ROW_TILE
    return pl.pallas_call(
        _experts_kernel,
        out_shape=jax.ShapeDtypeStruct((n_slots * ROW_TILE, LANES), F32),
        grid_spec=pltpu.PrefetchScalarGridSpec(
            num_scalar_prefetch=2,
            grid=(n_slots // tm,),
            in_specs=[pl.BlockSpec((tm * ROW_TILE, LANES), lambda g, te, nt: (g, 0)),
                      pl.BlockSpec((1, D_MODEL, D_EXPERT), lambda g, te, nt: (te[g], 0, 0)),
                      pl.BlockSpec((1, D_MODEL, D_EXPERT), lambda g, te, nt: (te[g], 0, 0)),
                      pl.BlockSpec((1, D_EXPERT, D_MODEL), lambda g, te, nt: (te[g], 0, 0))],
            out_specs=pl.BlockSpec((tm * ROW_TILE, LANES), lambda g, te, nt: (g, 0)),
            scratch_shapes=[pltpu.VMEM((D_MODEL, 2 * D_EXPERT), BF16), pltpu.VMEM((D_EXPERT, D_MODEL), BF16)]),
        compiler_params=_params(("arbitrary",)),
        name="routed_experts",
    )(tile_expert, n_tiles, xs, w_gate, w_up, w_down)


def _combine_kernel(dest_ref, dnext_ref, wsel_ref, h_ref, sg_ref, su_ref, sd_ref, g_ref, b_ref, ys_hbm,
                    out_ref, yg_ref, mix_ref, sem, *, tc):
    i = pl.program_id(0)
    last = pl.num_programs(0) - 1

    def fetch(d_ref, buf, wait):
        def body(g, carry):
            r0 = g * DMA_GROUP
            slots = [[0 if wait else d_ref[(r0 + u) * ROUTE_COLS + k] for k in range(TOP_K)]
                     for u in range(DMA_GROUP)]
            for u in range(DMA_GROUP):
                for k in range(TOP_K):
                    cp = _row_copy(ys_hbm, slots[u][k], yg_ref.at[buf, k], (r0 + u) * ROW_TILE, sem.at[buf])
                    cp.wait() if wait else cp.start(priority=k % 2)
            return carry
        lax.fori_loop(0, tc // DMA_GROUP, body, 0)

    cur = i % 2

    @pl.when(i == 0)
    def _():
        fetch(dest_ref, 0, False)

    @pl.when(jnp.logical_and(i < last, cur == 0))
    def _():
        fetch(dnext_ref, 1, False)

    @pl.when(jnp.logical_and(i < last, cur == 1))
    def _():
        fetch(dnext_ref, 0, False)

    h = h_ref[...]
    hb = _bf(h)
    shared = _dot(_bf(_silu(_dot(hb, sg_ref[...])) * _dot(hb, su_ref[...])), sd_ref[...])

    def finish(buf):
        fetch(dest_ref, buf, True)

        def mix(r, carry):
            row = pl.multiple_of(r * ROW_TILE, ROW_TILE)
            t = wsel_ref[r * ROUTE_COLS] * yg_ref[buf, 0, pl.ds(row, ROW_TILE), :]
            for k in range(1, TOP_K):
                t = t + wsel_ref[r * ROUTE_COLS + k] * yg_ref[buf, k, pl.ds(row, ROW_TILE), :]
            mix_ref[pl.ds(row, ROW_TILE), :] = t
            return carry

        lax.fori_loop(0, tc, mix, 0, unroll=4)
        acc = ALPHA * h + shared + _from_row_tiles(mix_ref, tc)
        out_ref[...] = _layer_norm(acc, g_ref[...], b_ref[...])

    @pl.when(cur == 0)
    def _():
        finish(0)

    @pl.when(cur == 1)
    def _():
        finish(1)


def _combine(h2d, ys, dest_flat, wsel_flat, sg_bf, su_bf, sd_bf, ln_g, ln_b, tc):
    n = h2d.shape[0]
    nblk = n // tc
    const = lambda shape: pl.BlockSpec(shape, lambda i: (0,) * len(shape))
    idx = lambda f: pl.BlockSpec((tc * ROUTE_COLS,), f, memory_space=pltpu.SMEM)
    return pl.pallas_call(
        functools.partial(_combine_kernel, tc=tc),
        out_shape=jax.ShapeDtypeStruct((n, D_MODEL), F32),
        grid=(nblk,),
        in_specs=[idx(lambda i: (i,)), idx(lambda i: (jnp.minimum(i + 1, nblk - 1),)), idx(lambda i: (i,)),
                  pl.BlockSpec((tc, D_MODEL), lambda i: (i, 0)),
                  const((D_MODEL, D_EXPERT)), const((D_MODEL, D_EXPERT)), const((D_EXPERT, D_MODEL)),
                  const((1, D_MODEL)), const((1, D_MODEL)),
                  pl.BlockSpec(memory_space=pl.ANY)],
        out_specs=pl.BlockSpec((tc, D_MODEL), lambda i: (i, 0)),
        scratch_shapes=[pltpu.VMEM((2, TOP_K, tc * ROW_TILE, LANES), F32),
                        pltpu.VMEM((tc * ROW_TILE, LANES), F32), pltpu.SemaphoreType.DMA((2,))],
        compiler_params=_params(("arbitrary",)),
        name="route_combine",
    )(dest_flat, dest_flat, wsel_flat, h2d, sg_bf, su_bf, sd_bf, ln_g, ln_b, ys)


def _moe(h2d, ht, w, *, tm):
    n = h2d.shape[0]
    n_tiles_max = (n * TOP_K) // tm + N_EXPERTS
    eidx, wsel, counts = _router(h2d, w["w_router"], w["e_bias"], _row_tile(n, 512))
    cnt = counts[0, :N_EXPERTS].astype(jnp.int32)
    tiles = (cnt + tm - 1) // tm
    tile_end = jnp.cumsum(tiles)
    off = ((tile_end - tiles) * tm).astype(F32)
    tile_expert = jnp.minimum(
        jnp.sum(tile_end[None, :] <= jnp.arange(n_tiles_max, dtype=jnp.int32)[:, None], axis=1),
        N_EXPERTS - 1).astype(jnp.int32)
    dest = _slots(eidx, jnp.pad(off, (0, LANES - N_EXPERTS))[None, :], _row_tile(n, 256))
    dest_flat = dest.reshape(n * ROUTE_COLS)
    xs = _scatter_rows(ht, dest_flat, n_tiles_max * tm, _row_tile(n, 128))
    ys = _experts(xs, tile_expert, tile_end[N_EXPERTS - 1:], w["w_gate"], w["w_up"], w["w_down"], tm)
    return _combine(h2d, ys, dest_flat, wsel.reshape(n * ROUTE_COLS), w["ws_gate"], w["ws_up"], w["ws_down"],
                    w["ln2_g"], w["ln2_b"], _row_tile(n, 128))


def _row_tile(n, cap):
    t = cap
    while n % t:
        t //= 2
    return t


def _trunk(x, s_hg, s_rw, shift, w, *, hg_cfg, rw_tb, moe_tm):
    B, T, _ = x.shape
    n = B * T
    x2d = x.reshape(n, D_MODEL)
    p_hg, p_rw = _proj(x2d, w["w_in"], _row_tile(n, 512))
    p_hg = p_hg.reshape(B, T, HG_COLS)
    p_rw = p_rw.reshape(B, T, RW_COLS)
    hg_out, hg_state = _hgrn(p_hg, s_hg, w["hg_lb"], w["hg_norm_g"], **hg_cfg)
    rw_out, rw_state = _rwkv(p_rw, shift, s_rw, w["rw_mu"], w["rw_w0"], w["rw_wa"], w["rw_a0"],
                             w["rw_g2"], w["rw_k_k"], w["rw_k_a"], w["rw_r_k"], w["rw_gn_g"],
                             w["rw_gn_b"], tb=rw_tb)
    h, ht = _outproj(x2d, hg_out.reshape(n, HG_WIDTH), rw_out.reshape(n, RW_WIDTH), w["w_out"],
                     w["ln1_g"], w["ln1_b"], _row_tile(n, 512))
    y = _moe(h, ht, w, tm=moe_tm)
    return y.reshape(B, T, D_MODEL), hg_state, rw_state, p_rw[:, T - 1, :]


def kernel(x_prompt, x_sample, state_hgrn, state_rwkv, state_shift, w_in, w_out, hg_lb, hg_norm_g, rw_mu, rw_w0, rw_w2, rw_a0, rw_a2, rw_g2, rw_k_k, rw_k_a, rw_r_k, rw_gn_g, rw_gn_b, ln1_g, ln1_b, ln2_g, ln2_b, w_router, e_bias, w_gate, w_up, w_down, ws_gate, ws_up, ws_down):
    l = 0
    zeros_lora = jnp.zeros((RW_DH, RW_WIDTH), F32)
    w = {
        "w_in": _bf(w_in[l]), "w_out": _bf(w_out[l]),
        "hg_lb": hg_lb, "hg_norm_g": hg_norm_g[l][None, :],
        "rw_mu": rw_mu[l][None, :], "rw_w0": rw_w0[l][None, :], "rw_a0": rw_a0[l][None, :],
        "rw_wa": _bf(jnp.concatenate([jnp.concatenate([rw_w2[l], zeros_lora], axis=1),
                                      jnp.concatenate([zeros_lora, rw_a2[l]], axis=1)], axis=0)),
        "rw_g2": _bf(rw_g2[l]),
        "rw_k_k": rw_k_k[l][None, :], "rw_k_a": rw_k_a[l][None, :],
        "rw_r_k": rw_r_k[l].reshape(1, RW_WIDTH),
        "rw_gn_g": rw_gn_g[l][None, :], "rw_gn_b": rw_gn_b[l][None, :],
        "ln1_g": ln1_g[l][None, :], "ln1_b": ln1_b[l][None, :],
        "ln2_g": ln2_g[l][None, :], "ln2_b": ln2_b[l][None, :],
        "w_router": jnp.pad(w_router[l], ((0, 0), (0, LANES - N_EXPERTS))),
        "e_bias": jnp.pad(e_bias[l], (0, LANES - N_EXPERTS))[None, :],
        "w_gate": w_gate[l], "w_up": w_up[l], "w_down": w_down[l],
        "ws_gate": _bf(ws_gate[l]), "ws_up": _bf(ws_up[l]), "ws_down": _bf(ws_down[l]),
    }
    Bp = x_prompt.shape[0]
    zero_hg = jnp.zeros((Bp, HG_HEADS, HG_DK, HG_DK), F32)
    zero_rw = jnp.zeros((Bp, RW_HEADS, RW_DH, RW_DH), F32)
    zero_sh = jnp.zeros((Bp, RW_COLS), F32)
    yp, hg_p, rw_p, sh_p = _trunk(x_prompt, zero_hg, zero_rw, zero_sh, w,
                                  hg_cfg=dict(Bb=1, Tb=128, C=64, R=16), rw_tb=64, moe_tm=512)
    ys, hg_s, rw_s, sh_s = _trunk(x_sample, state_hgrn[l], state_rwkv[l], state_shift[l], w,
                                  hg_cfg=dict(Bb=8, Tb=8, C=8, R=8), rw_tb=8, moe_tm=128)
    return (yp, ys, hg_p[None], rw_p[None], sh_p[None], hg_s[None], rw_s[None], sh_s[None])
```

```python
import functools

import jax
import jax.numpy as jnp
from jax import lax
from jax.experimental import pallas as pl
from jax.experimental.pallas import tpu as pltpu

F32 = jnp.float32
BF16 = jnp.bfloat16

D_MODEL = 1024
HG_HEADS = 4
HG_DK = 128
HG_WIDTH = 512
HG_COLS = 2048
RW_HEADS = 8
RW_DH = 64
RW_WIDTH = 512
RW_COLS = 1792
N_EXPERTS = 64
TOP_K = 6
N_GROUPS = 8
TOPK_GROUPS = 4
GROUP_SIZE = N_EXPERTS // N_GROUPS
D_EXPERT = 256
ROUTED_SCALE = 2.5
DEPTH = 1
ALPHA = (2.0 * DEPTH) ** 0.25
LN_EPS = 1e-5
HEAD_NORM_EPS = 1e-6
RW_GN_EPS = 64e-5

LANES = 128
VMEM_LIMIT = 56 * 1024 * 1024


def _bf(x):
    return x.astype(BF16)


def _dot(a, b):
    return jnp.dot(a, b, preferred_element_type=F32)


def _dot_nt(a, b):
    return lax.dot_general(a, b, (((1,), (1,)), ((), ())), preferred_element_type=F32)


def _split2(x):
    hi = _bf(x)
    lo = _bf(x - hi.astype(F32))
    return hi, lo


def _split3(x):
    hi = _bf(x)
    r1 = x - hi.astype(F32)
    mid = _bf(r1)
    lo = _bf(r1 - mid.astype(F32))
    return hi, mid, lo


def _sigmoid(x):
    return 1.0 / (1.0 + jnp.exp(-x))


def _silu(x):
    return x * _sigmoid(x)


def _params(sem):
    return pltpu.CompilerParams(dimension_semantics=sem, vmem_limit_bytes=VMEM_LIMIT)


def _proj_kernel(x_ref, w_ref, hg_ref, rw_ref):
    xb = _bf(x_ref[...])
    step = 256
    for c0 in range(0, HG_COLS, step):
        hg_ref[:, c0:c0 + step] = _dot(xb, w_ref[:, c0:c0 + step])
    for c0 in range(0, RW_COLS, step):
        rw_ref[:, c0:c0 + step] = _dot(xb, w_ref[:, HG_COLS + c0:HG_COLS + c0 + step])


def _proj(x2d, w_in_bf, tm):
    n = x2d.shape[0]
    return pl.pallas_call(
        _proj_kernel,
        out_shape=(jax.ShapeDtypeStruct((n, HG_COLS), F32), jax.ShapeDtypeStruct((n, RW_COLS), F32)),
        grid=(n // tm,),
        in_specs=[pl.BlockSpec((tm, D_MODEL), lambda i: (i, 0)),
                  pl.BlockSpec((D_MODEL, HG_COLS + RW_COLS), lambda i: (0, 0))],
        out_specs=(pl.BlockSpec((tm, HG_COLS), lambda i: (i, 0)),
                   pl.BlockSpec((tm, RW_COLS), lambda i: (i, 0))),
        compiler_params=_params(("parallel",)),
        name="in_proj",
    )(x2d, w_in_bf)


def _hgrn_kernel(p_ref, s0_ref, lbraw_ref, ng_ref, out_ref, sfin_ref, st_ref, *, Bb, Tb, C, R):
    ti = pl.program_id(1)
    nt = pl.num_programs(1)
    nb = C // R

    @pl.when(ti == 0)
    def _():
        for b in range(Bb):
            for h in range(HG_HEADS):
                st_ref[b, h] = s0_ref[b, h].T

    lbr = lbraw_ref[...]
    lbe = jnp.exp(lbr - jnp.max(lbr, axis=0, keepdims=True))
    lb = lbe[0:1, :] / jnp.sum(lbe, axis=0, keepdims=True)
    ng = ng_ref[...]

    if Tb > 8:
        rows = lax.broadcasted_iota(jnp.int32, (Tb, Tb), 0)
        cols = lax.broadcasted_iota(jnp.int32, (Tb, Tb), 1)
        tril = _bf(jnp.where(((rows // C) == (cols // C)) & (cols <= rows), 1.0, 0.0))
    tidx = lax.broadcasted_iota(jnp.int32, (R, 1), 0)
    cidx = lax.broadcasted_iota(jnp.int32, (C, 1), 0)

    for b in range(Bb):
        qs = _silu(p_ref[b, :, 0:512])
        f = lb + (1.0 - lb) * _sigmoid(p_ref[b, :, 512:1024])
        lf = jnp.log(f)
        kd = 1.0 - f
        v = p_ref[b, :, 1024:1536]
        if Tb == 8:
            r8 = lax.broadcasted_iota(jnp.int32, (8, 1), 0)
            bcum = jnp.concatenate(
                [jnp.sum(jnp.where(r8 <= t, lf, 0.0), axis=0, keepdims=True) for t in range(8)], axis=0)
        else:
            hi, mid, lo = _split3(lf)
            bcum = _dot(tril, hi) + _dot(tril, mid) + _dot(tril, lo)
        o_heads = []
        for h in range(HG_HEADS):
            sl = slice(h * HG_DK, (h + 1) * HG_DK)
            st = st_ref[b, h]
            o_chunks = []
            for c in range(Tb // C):
                r0 = c * C
                bc = bcum[r0:r0 + C, sl]
                qc = qs[r0:r0 + C, sl]
                kc = kd[r0:r0 + C, sl]
                vc = v[r0:r0 + C, sl]
                b_last = bc[C - 1:C, :]
                o_inter = _dot_nt(_bf(qc * jnp.exp(bc)), _bf(st))
                o_blocks = []
                for i in range(nb):
                    rs = slice(i * R, (i + 1) * R)
                    bq = bc[rs]
                    qq = qc[rs]
                    acc = jnp.zeros((R, HG_DK), F32)
                    if i > 0:
                        anchor = bc[i * R - 1:i * R, :]
                        early = cidx < i * R
                        qa = qq * jnp.exp(bq - anchor)
                        ka = jnp.where(early, kc * jnp.exp(jnp.where(early, anchor - bc, 0.0)), 0.0)
                        att = _dot_nt(_bf(qa), _bf(ka))
                        acc = acc + _dot(_bf(att), _bf(vc))
                    for s in range(R):
                        sa = i * R + s
                        msk = tidx >= s
                        e = jnp.exp(jnp.where(msk, bq - bc[sa:sa + 1, :], 0.0))
                        term = jnp.where(msk, qq * kc[sa:sa + 1, :] * e, 0.0)
                        acc = acc + jnp.sum(term, axis=-1, keepdims=True) * vc[sa:sa + 1, :]
                    o_blocks.append(acc)
                o_intra = o_blocks[0] if nb == 1 else jnp.concatenate(o_blocks, axis=0)
                o_chunks.append(o_inter + o_intra)
                k_out = kc * jnp.exp(b_last - bc)
                st = st * jnp.exp(b_last) + _dot(_bf(vc.T), _bf(k_out))
            st_ref[b, h] = st
            o_h = o_chunks[0] if len(o_chunks) == 1 else jnp.concatenate(o_chunks, axis=0)
            o_h = o_h * lax.rsqrt(jnp.mean(o_h * o_h, axis=-1, keepdims=True) + HEAD_NORM_EPS) * ng[:, sl]
            o_heads.append(o_h)
        out_ref[b] = jnp.concatenate(o_heads, axis=-1) * _silu(p_ref[b, :, 1536:2048])

    @pl.when(ti == nt - 1)
    def _():
        for b in range(Bb):
            for h in range(HG_HEADS):
                sfin_ref[b, h] = st_ref[b, h].T


def _hgrn(p_hg, s0, lb_raw, norm_g, *, Bb, Tb, C, R):
    B, T, _ = p_hg.shape
    kern = functools.partial(_hgrn_kernel, Bb=Bb, Tb=Tb, C=C, R=R)
    return pl.pallas_call(
        kern,
        out_shape=(jax.ShapeDtypeStruct((B, T, HG_WIDTH), F32),
                   jax.ShapeDtypeStruct((B, HG_HEADS, HG_DK, HG_DK), F32)),
        grid=(B // Bb, T // Tb),
        in_specs=[pl.BlockSpec((Bb, Tb, HG_COLS), lambda i, j: (i, j, 0)),
                  pl.BlockSpec((Bb, HG_HEADS, HG_DK, HG_DK), lambda i, j: (i, 0, 0, 0)),
                  pl.BlockSpec((DEPTH + 1, HG_WIDTH), lambda i, j: (0, 0)),
                  pl.BlockSpec((1, HG_WIDTH), lambda i, j: (0, 0))],
        out_specs=(pl.BlockSpec((Bb, Tb, HG_WIDTH), lambda i, j: (i, j, 0)),
                   pl.BlockSpec((Bb, HG_HEADS, HG_DK, HG_DK), lambda i, j: (i, 0, 0, 0))),
        scratch_shapes=[pltpu.VMEM((Bb, HG_HEADS, HG_DK, HG_DK), F32)],
        compiler_params=_params(("parallel", "arbitrary")),
        name="hgrn2_mixer",
    )(p_hg, s0, lb_raw, norm_g)


RW_PAIRS = RW_HEADS // 2
EXP_NEG_HALF = 0.6065306597126334
RW_SEQ = 8
RW_AHEAD = 6


def _block_ones():
    i = lax.broadcasted_iota(jnp.int32, (LANES, LANES), 0)
    j = lax.broadcasted_iota(jnp.int32, (LANES, LANES), 1)
    return _bf(jnp.where((i // RW_DH) == (j // RW_DH), 1.0, 0.0))


def _seg_sum(x, bd):
    outs = []
    for c in range(RW_WIDTH // LANES):
        hi, lo = _split2(x[:, c * LANES:(c + 1) * LANES])
        outs.append(_dot(hi, bd) + _dot(lo, bd))
    return jnp.concatenate(outs, axis=-1)


def _rwkv_kernel(p_ref, shift_ref, s0_ref, mu_ref, w0_ref, wa_ref, a0_ref, g2_ref, kk_ref, ka_ref,
                 rk_ref, gng_ref, gnb_ref, out_ref, sfin_ref,
                 s_ref, prev_ref, r_s, w_s, k_s, v_s, kk_s, kka_s, g_s, vt_s, o_s, *, tb):
    ti = pl.program_id(1)
    nt = pl.num_programs(1)
    pairs = range(RW_PAIRS)
    sub = 8

    @pl.when(ti == 0)
    def _():
        prev_ref[...] = shift_ref[...]
        for b in range(RW_SEQ):
            for p in pairs:
                s_ref[b, p * RW_DH:(p + 1) * RW_DH, :] = jnp.concatenate(
                    [s0_ref[b, 2 * p], s0_ref[b, 2 * p + 1]], axis=-1)

    bd = _block_ones()
    lane = lax.broadcasted_iota(jnp.int32, (1, LANES), 1)
    rid = lax.broadcasted_iota(jnp.int32, (tb, 1), 0)
    mu = mu_ref[...]

    for b in range(RW_SEQ):
        p = p_ref[b]
        prev = jnp.where(rid == 0, prev_ref[b:b + 1, :], pltpu.roll(p, 1, axis=0))
        prev_ref[b:b + 1, :] = p[tb - 1:tb, :]
        ps = p + mu * (prev - p)
        r = ps[:, 0:512]
        k = ps[:, 512:1024]
        v = ps[:, 1024:1536]
        zz = ps[:, 1536:1664]
        zg = ps[:, 1664:1792]
        lora = _dot(_bf(jnp.where(lane < 64, jnp.tanh(zz), zz)), wa_ref[...])
        exp_w = EXP_NEG_HALF / (1.0 + jnp.exp(-(w0_ref[...] + lora[:, 0:512])))
        a = _sigmoid(a0_ref[...] + lora[:, 512:1024])
        kk = k * kk_ref[...]
        kk = kk / jnp.maximum(jnp.sqrt(_seg_sum(kk * kk, bd)), 1e-12)
        r_s[b] = r
        w_s[b] = jnp.exp(-exp_w)
        k_s[b] = k * (1.0 + (a - 1.0) * ka_ref[...])
        v_s[b] = v
        kk_s[b] = kk
        kka_s[b] = kk * a
        g_s[b] = _dot(_bf(_sigmoid(zg)), g2_ref[...])
        for q in pairs:
            vp = v[:, q * LANES:(q + 1) * LANES]
            if tb < RW_DH:
                vp = jnp.concatenate([vp, jnp.zeros((RW_DH - tb, LANES), F32)], axis=0)
            vpt = vp.T
            vt_s[b, q * RW_DH:(q + 1) * RW_DH, :] = _bf(
                jnp.concatenate([vpt[0:RW_DH, :], vpt[RW_DH:LANES, :]], axis=1))

    ci = lax.broadcasted_iota(jnp.int32, (LANES, LANES), 0)
    cj = lax.broadcasted_iota(jnp.int32, (LANES, LANES), 1)
    same_head = (ci // RW_DH) == (cj // RW_DH)
    si = lax.broadcasted_iota(jnp.int32, (16, 2 * LANES), 0)
    sj = lax.broadcasted_iota(jnp.int32, (16, 2 * LANES), 1)
    head_rows = _bf(jnp.where(si == sj // RW_DH, 1.0, 0.0))

    def rows(ref, b, t0, j):
        return [ref[b, pl.ds(t0, sub), pl.ds(q * LANES, LANES)][j:j + 1, :] for q in pairs]

    def scale(x, rws):
        return jnp.concatenate([x[q * RW_DH:(q + 1) * RW_DH] * rws[q] for q in pairs], axis=0)

    def side_by_side(x):
        return jnp.concatenate(
            [jnp.concatenate([x[0:RW_DH], x[RW_DH:2 * RW_DH]], axis=1),
             jnp.concatenate([x[2 * RW_DH:3 * RW_DH], x[3 * RW_DH:4 * RW_DH]], axis=1)], axis=0)

    def steps(tq, carry):
        t0 = pl.multiple_of(tq * sub, sub)
        for j in range(sub):
            tl = tq * sub + j
            pick = _bf(jnp.where(same_head & ((ci % RW_DH) == tl), 1.0, 0.0))
            def reductions(b):
                return (_dot(_bf(scale(s_ref[b], rows(kk_s, b, t0, j))), bd), _dot(vt_s[b], pick))

            ahead = [reductions(b) for b in range(RW_AHEAD)]
            for b in range(RW_SEQ):
                sa, vb = ahead.pop(0)
                if b + RW_AHEAD < RW_SEQ:
                    ahead.append(reductions(b + RW_AHEAD))
                s = (scale(s_ref[b], rows(w_s, b, t0, j)) - scale(sa, rows(kka_s, b, t0, j))
                     + scale(vb, rows(k_s, b, t0, j)))
                s_ref[b] = s
                o4 = _dot_nt(head_rows, _bf(side_by_side(scale(s, rows(r_s, b, t0, j)))))
                for h in range(RW_PAIRS):
                    o_s[h, tq, j:j + 1, b * LANES:(b + 1) * LANES] = o4[h:h + 1, :]
        return carry

    lax.fori_loop(0, tb // sub, steps, 0)

    for b in range(RW_SEQ):
        lanes = slice(b * LANES, (b + 1) * LANES)
        nq = tb // sub
        oh = [jnp.concatenate([o_s[h, q, :, lanes] for q in range(nq)], axis=0) if nq > 1
              else o_s[h, 0, :, lanes] for h in range(RW_PAIRS)]
        o = jnp.concatenate([oh[h % RW_PAIRS][:, (h // RW_PAIRS) * RW_DH:(h // RW_PAIRS + 1) * RW_DH]
                             for h in range(RW_HEADS)], axis=-1)
        d = o - _seg_sum(o, bd) * (1.0 / RW_DH)
        var = _seg_sum(d * d, bd) * (1.0 / RW_DH)
        on = d * lax.rsqrt(var + RW_GN_EPS) * gng_ref[...] + gnb_ref[...]
        bonus = _seg_sum(r_s[b] * k_s[b] * rk_ref[...], bd) * v_s[b]
        out_ref[b] = (on + bonus) * g_s[b]

    @pl.when(ti == nt - 1)
    def _():
        for b in range(RW_SEQ):
            for p in pairs:
                s = s_ref[b, p * RW_DH:(p + 1) * RW_DH, :]
                sfin_ref[b, 2 * p] = s[:, 0:RW_DH]
                sfin_ref[b, 2 * p + 1] = s[:, RW_DH:LANES]


def _rwkv(p_rw, shift, s0, mu, w0, wa, a0, g2, k_k, k_a, r_k, gn_g, gn_b, *, tb):
    B, T, _ = p_rw.shape
    kern = functools.partial(_rwkv_kernel, tb=tb)
    vec = lambda n: pl.BlockSpec((1, n), lambda i, j: (0, 0))
    blk = pltpu.VMEM((RW_SEQ, tb, RW_WIDTH), F32)
    oblk = pltpu.VMEM((RW_PAIRS, tb // 8, 8, RW_SEQ * LANES), F32)
    return pl.pallas_call(
        kern,
        out_shape=(jax.ShapeDtypeStruct((B, T, RW_WIDTH), F32),
                   jax.ShapeDtypeStruct((B, RW_HEADS, RW_DH, RW_DH), F32)),
        grid=(B // RW_SEQ, T // tb),
        in_specs=[pl.BlockSpec((RW_SEQ, tb, RW_COLS), lambda i, j: (i, j, 0)),
                  pl.BlockSpec((RW_SEQ, RW_COLS), lambda i, j: (i, 0)),
                  pl.BlockSpec((RW_SEQ, RW_HEADS, RW_DH, RW_DH), lambda i, j: (i, 0, 0, 0)),
                  vec(RW_COLS), vec(RW_WIDTH),
                  pl.BlockSpec((LANES, 2 * RW_WIDTH), lambda i, j: (0, 0)),
                  vec(RW_WIDTH),
                  pl.BlockSpec((LANES, RW_WIDTH), lambda i, j: (0, 0)),
                  vec(RW_WIDTH), vec(RW_WIDTH), vec(RW_WIDTH), vec(RW_WIDTH), vec(RW_WIDTH)],
        out_specs=(pl.BlockSpec((RW_SEQ, tb, RW_WIDTH), lambda i, j: (i, j, 0)),
                   pl.BlockSpec((RW_SEQ, RW_HEADS, RW_DH, RW_DH), lambda i, j: (i, 0, 0, 0))),
        scratch_shapes=[pltpu.VMEM((RW_SEQ, RW_PAIRS * RW_DH, LANES), F32),
                        pltpu.VMEM((RW_SEQ, RW_COLS), F32),
                        blk, blk, blk, blk, blk, blk, blk,
                        pltpu.VMEM((RW_SEQ, RW_PAIRS * RW_DH, LANES), BF16),
                        oblk],
        compiler_params=_params(("parallel", "arbitrary")),
        name="rwkv7_mixer",
    )(p_rw, shift, s0, mu, w0, wa, a0, g2, k_k, k_a, r_k, gn_g, gn_b)


def _layer_norm(y, g, b):
    mu = jnp.mean(y, axis=-1, keepdims=True)
    d = y - mu
    var = jnp.mean(d * d, axis=-1, keepdims=True)
    return d * lax.rsqrt(var + LN_EPS) * g + b


ROW_TILE = 8


def _to_row_tiles(ref, x):
    rows = x.shape[0]
    for s in range(ROW_TILE):
        ref[pl.ds(s, rows, stride=ROW_TILE), :] = x[:, s * LANES:(s + 1) * LANES]


def _from_row_tiles(ref, rows):
    return jnp.concatenate([ref[pl.ds(s, rows, stride=ROW_TILE), :] for s in range(ROW_TILE)], axis=-1)


def _outproj_kernel(x_ref, hg_ref, rw_ref, w_ref, g_ref, b_ref, h_ref, ht_ref):
    mix = (_dot(_bf(hg_ref[...]), w_ref[0:HG_WIDTH, :])
           + _dot(_bf(rw_ref[...]), w_ref[HG_WIDTH:HG_WIDTH + RW_WIDTH, :]))
    h = _layer_norm(ALPHA * x_ref[...] + mix, g_ref[...], b_ref[...])
    h_ref[...] = h
    _to_row_tiles(ht_ref, h)


def _outproj(x2d, hg2d, rw2d, w_out_bf, ln_g, ln_b, tm):
    n = x2d.shape[0]
    rows = lambda w: pl.BlockSpec((tm, w), lambda i: (i, 0))
    return pl.pallas_call(
        _outproj_kernel,
        out_shape=(jax.ShapeDtypeStruct((n, D_MODEL), F32),
                   jax.ShapeDtypeStruct((n * ROW_TILE, LANES), F32)),
        grid=(n // tm,),
        in_specs=[rows(D_MODEL), rows(HG_WIDTH), rows(RW_WIDTH),
                  pl.BlockSpec((D_MODEL, D_MODEL), lambda i: (0, 0)),
                  pl.BlockSpec((1, D_MODEL), lambda i: (0, 0)),
                  pl.BlockSpec((1, D_MODEL), lambda i: (0, 0))],
        out_specs=(rows(D_MODEL), pl.BlockSpec((tm * ROW_TILE, LANES), lambda i: (i, 0))),
        compiler_params=_params(("parallel",)),
        name="out_proj_ln",
    )(x2d, hg2d, rw2d, w_out_bf, ln_g, ln_b)


ROUTE_COLS = 8


def _router_kernel(h_ref, w_ref, eb_ref, eidx_ref, wsel_ref, cnt_ref):
    tm = h_ref.shape[0]
    hh, hl = _split2(h_ref[...])
    wh, wl = _split2(w_ref[...])
    logits = _dot(hh, wh) + (_dot(hh, wl) + _dot(hl, wh))
    lane = lax.broadcasted_iota(jnp.int32, (tm, LANES), 1)
    valid = lane < N_EXPERTS
    neg = -jnp.inf
    scores = _sigmoid(logits)
    choice = jnp.where(valid, scores + eb_ref[...], neg)

    def partner(x, s):
        return jnp.where((lane & s) != 0, pltpu.roll(x, s, axis=1), pltpu.roll(x, LANES - s, axis=1))

    def group_max(x):
        for s in (1, 2, 4):
            x = jnp.maximum(x, partner(x, s))
        return x

    def group_min(x):
        for s in (1, 2, 4):
            x = jnp.minimum(x, partner(x, s))
        return x

    m1 = group_max(choice)
    first = group_min(jnp.where(choice == m1, lane, LANES))
    m2 = group_max(jnp.where(lane == first, neg, choice))
    gscore = m1 + m2
    gper = jnp.where(valid, gscore, pltpu.roll(gscore, N_EXPERTS, axis=1))
    gidx = (lane % N_EXPERTS) // GROUP_SIZE
    rank = jnp.zeros((tm, LANES), jnp.int32)
    for j in range(1, N_GROUPS):
        other = pltpu.roll(gper, GROUP_SIZE * j, axis=1)
        beats = (other > gper) | ((other == gper) & (gidx >= j))
        rank = rank + beats.astype(jnp.int32)
    cand = jnp.where(valid & (rank < TOPK_GROUPS), choice, neg)
    sel = jnp.zeros((tm, LANES), jnp.bool_)
    hits = []
    eidx = jnp.zeros((tm, LANES), jnp.int32)
    for k in range(TOP_K):
        m = jnp.max(cand, axis=-1, keepdims=True)
        idx = jnp.min(jnp.where(cand == m, lane, LANES), axis=-1, keepdims=True)
        hit = lane == idx
        hits.append(hit)
        eidx = jnp.where(lane == k, idx, eidx)
        sel = sel | hit
        cand = jnp.where(hit, neg, cand)
    wts = jnp.where(sel, scores, 0.0)
    gates = wts / jnp.sum(wts, axis=-1, keepdims=True) * ROUTED_SCALE
    wsel = jnp.zeros((tm, LANES), F32)
    for k in range(TOP_K):
        wsel = jnp.where(lane == k, jnp.sum(jnp.where(hits[k], gates, 0.0), axis=-1, keepdims=True), wsel)
    eidx_ref[...] = eidx[:, 0:ROUTE_COLS]
    wsel_ref[...] = wsel[:, 0:ROUTE_COLS]

    @pl.when(pl.program_id(0) == 0)
    def _():
        cnt_ref[...] = jnp.zeros((1, LANES), F32)

    cnt_ref[...] += jnp.sum(jnp.where(sel, 1.0, 0.0), axis=0, keepdims=True)


def _router(h2d, w_router_pad, e_bias_pad, tm):
    n = h2d.shape[0]
    return pl.pallas_call(
        _router_kernel,
        out_shape=(jax.ShapeDtypeStruct((n, ROUTE_COLS), jnp.int32),
                   jax.ShapeDtypeStruct((n, ROUTE_COLS), F32),
                   jax.ShapeDtypeStruct((1, LANES), F32)),
        grid=(n // tm,),
        in_specs=[pl.BlockSpec((tm, D_MODEL), lambda i: (i, 0)),
                  pl.BlockSpec((D_MODEL, LANES), lambda i: (0, 0)),
                  pl.BlockSpec((1, LANES), lambda i: (0, 0))],
        out_specs=(pl.BlockSpec((tm, ROUTE_COLS), lambda i: (i, 0)),
                   pl.BlockSpec((tm, ROUTE_COLS), lambda i: (i, 0)),
                   pl.BlockSpec((1, LANES), lambda i: (0, 0))),
        compiler_params=_params(("arbitrary",)),
        name="router",
    )(h2d, w_router_pad, e_bias_pad)


def _slot_kernel(eidx_ref, off_ref, dest_ref, carry_ref):
    tb = eidx_ref.shape[0]

    @pl.when(pl.program_id(0) == 0)
    def _():
        carry_ref[...] = jnp.zeros((1, LANES), F32)

    lane = lax.broadcasted_iota(jnp.int32, (tb, LANES), 1)
    eidx = eidx_ref[...]
    onehot = [lane == eidx[:, k:k + 1] for k in range(TOP_K)]
    member = jnp.zeros((tb, LANES), F32)
    for k in range(TOP_K):
        member = member + jnp.where(onehot[k], 1.0, 0.0)
    ri = lax.broadcasted_iota(jnp.int32, (tb, tb), 0)
    rj = lax.broadcasted_iota(jnp.int32, (tb, tb), 1)
    before = _bf(jnp.where(rj < ri, 1.0, 0.0))
    slot = _dot(before, _bf(member)) + carry_ref[...] + off_ref[...]
    dest = jnp.zeros((tb, LANES), F32)
    for k in range(TOP_K):
        dest = jnp.where(lane == k, jnp.sum(jnp.where(onehot[k], slot, 0.0), axis=-1, keepdims=True), dest)
    dest_ref[...] = dest[:, 0:ROUTE_COLS].astype(jnp.int32) * ROW_TILE
    carry_ref[...] += jnp.sum(member, axis=0, keepdims=True)


def _slots(eidx, off, tb):
    n = eidx.shape[0]
    return pl.pallas_call(
        _slot_kernel,
        out_shape=jax.ShapeDtypeStruct((n, ROUTE_COLS), jnp.int32),
        grid=(n // tb,),
        in_specs=[pl.BlockSpec((tb, ROUTE_COLS), lambda i: (i, 0)),
                  pl.BlockSpec((1, LANES), lambda i: (0, 0))],
        out_specs=pl.BlockSpec((tb, ROUTE_COLS), lambda i: (i, 0)),
        scratch_shapes=[pltpu.VMEM((1, LANES), F32)],
        compiler_params=_params(("arbitrary",)),
        name="route_slots",
    )(eidx, off)


DMA_GROUP = 4


def _row_copy(src_ref, src_row, dst_ref, dst_row, sem):
    return pltpu.make_async_copy(src_ref.at[pl.ds(pl.multiple_of(src_row, ROW_TILE), ROW_TILE)],
                                 dst_ref.at[pl.ds(pl.multiple_of(dst_row, ROW_TILE), ROW_TILE)], sem)


def _scatter_kernel(pad_ref, nt_ref, dest_ref, ht_ref, xs_hbm, zero_ref, sem, zsem, *, tb, tm, n_tiles_max):
    @pl.when(pl.program_id(0) == 0)
    def _():
        zero_ref[...] = jnp.zeros(zero_ref.shape, F32)

        def piece(rows, slot, wait):
            cp = pltpu.make_async_copy(
                zero_ref.at[pl.ds(0, rows * ROW_TILE)],
                xs_hbm.at[pl.ds(pl.multiple_of(slot * ROW_TILE, ROW_TILE), rows * ROW_TILE)], zsem)
            cp.wait() if wait else cp.start()

        def fill(wait):
            def pad(e, carry):
                slot = pad_ref[2 * e]
                length = pad_ref[2 * e + 1]
                bit = tm // 2
                while bit >= 1:
                    @pl.when((length & bit) != 0)
                    def _(bit=bit, slot=slot):
                        piece(bit, slot, wait)
                    slot = slot + (length & bit)
                    bit //= 2
                return carry

            def unused(g, carry):
                piece(tm, g * tm, wait)
                return carry

            lax.fori_loop(0, N_EXPERTS, pad, 0)
            lax.fori_loop(nt_ref[0], n_tiles_max, unused, 0)

        fill(False)
        fill(True)

    def start(g, carry):
        r0 = g * DMA_GROUP
        slots = [[dest_ref[(r0 + u) * ROUTE_COLS + k] for k in range(TOP_K)] for u in range(DMA_GROUP)]
        for u in range(DMA_GROUP):
            for k in range(TOP_K):
                _row_copy(ht_ref, (r0 + u) * ROW_TILE, xs_hbm, slots[u][k], sem).start(priority=k % 2)
        return carry

    def wait(r, carry):
        for k in range(TOP_K):
            _row_copy(ht_ref, 0, xs_hbm, 0, sem).wait()
        return carry

    lax.fori_loop(0, tb // DMA_GROUP, start, 0)
    lax.fori_loop(0, tb, wait, 0)


def _scatter_rows(ht, dest_flat, pad_info, n_tiles, n_tiles_max, tm, tb):
    n = ht.shape[0] // ROW_TILE
    return pl.pallas_call(
        functools.partial(_scatter_kernel, tb=tb, tm=tm, n_tiles_max=n_tiles_max),
        out_shape=jax.ShapeDtypeStruct((n_tiles_max * tm * ROW_TILE, LANES), F32),
        grid_spec=pltpu.PrefetchScalarGridSpec(
            num_scalar_prefetch=2,
            grid=(n // tb,),
            in_specs=[pl.BlockSpec((tb * ROUTE_COLS,), lambda i, pad, nt: (i,), memory_space=pltpu.SMEM),
                      pl.BlockSpec((tb * ROW_TILE, LANES), lambda i, pad, nt: (i, 0))],
            out_specs=pl.BlockSpec(memory_space=pl.ANY),
            scratch_shapes=[pltpu.VMEM((tm * ROW_TILE, LANES), F32),
                            pltpu.SemaphoreType.DMA, pltpu.SemaphoreType.DMA]),
        compiler_params=_params(("arbitrary",)),
        name="route_scatter",
    )(pad_info, n_tiles, dest_flat, ht)


XS_RING = 3


def _experts_kernel(te_ref, nt_ref, xs_hbm, wg_ref, wu_ref, wd_ref, ys_ref, wgu_bf, wd_bf, xs_buf, xs_sem, *, tm):
    g = pl.program_id(0)
    n_live = nt_ref[0]
    live = g < n_live

    def tile_copy(t):
        return pltpu.make_async_copy(
            xs_hbm.at[pl.ds(pl.multiple_of(t * (tm * ROW_TILE), ROW_TILE), tm * ROW_TILE)],
            xs_buf.at[t % XS_RING], xs_sem.at[t % XS_RING])

    @pl.when(g == 0)
    def _():
        for t in range(XS_RING - 1):
            @pl.when(t < n_live)
            def _(t=t):
                tile_copy(t).start()

    @pl.when(g + (XS_RING - 1) < n_live)
    def _():
        tile_copy(g + (XS_RING - 1)).start()

    @pl.when(jnp.logical_or(g == 0, te_ref[g] != te_ref[jnp.maximum(g - 1, 0)]))
    def _():
        wgu_bf[:, 0:D_EXPERT] = _bf(wg_ref[0])
        wgu_bf[:, D_EXPERT:2 * D_EXPERT] = _bf(wu_ref[0])
        wd_bf[...] = _bf(wd_ref[0])

    @pl.when(live)
    def _():
        tile_copy(g).wait()
        xb = _bf(_from_row_tiles(xs_buf.at[g % XS_RING], tm))
        gu = _dot(xb, wgu_bf[...])
        act = _silu(gu[:, 0:D_EXPERT]) * gu[:, D_EXPERT:2 * D_EXPERT]
        _to_row_tiles(ys_ref, _dot(_bf(act), wd_bf[...]))

    @pl.when(jnp.logical_not(live))
    def _():
        ys_ref[...] = jnp.zeros(ys_ref.shape, F32)


def _experts(xs, tile_expert, n_tiles, w_gate, w_up, w_down, tm):
    n_slots = xs.shape[0] // ROW_TILE
    return pl.pallas_call(
        functools.partial(_experts_kernel, tm=tm),
        out_shape=jax.ShapeDtypeStruct((n_slots * ROW_TILE, LANES), F32),
        grid_spec=pltpu.PrefetchScalarGridSpec(
            num_scalar_prefetch=2,
            grid=(n_slots // tm,),
            in_specs=[pl.BlockSpec(memory_space=pl.ANY),
                      pl.BlockSpec((1, D_MODEL, D_EXPERT), lambda g, te, nt: (te[g], 0, 0)),
                      pl.BlockSpec((1, D_MODEL, D_EXPERT), lambda g, te, nt: (te[g], 0, 0)),
                      pl.BlockSpec((1, D_EXPERT, D_MODEL), lambda g, te, nt: (te[g], 0, 0))],
            out_specs=pl.BlockSpec((tm * ROW_TILE, LANES), lambda g, te, nt: (g, 0)),
            scratch_shapes=[pltpu.VMEM((D_MODEL, 2 * D_EXPERT), BF16), pltpu.VMEM((D_EXPERT, D_MODEL), BF16),
                            pltpu.VMEM((XS_RING, tm * ROW_TILE, LANES), F32),
                            pltpu.SemaphoreType.DMA((XS_RING,))]),
        compiler_params=_params(("arbitrary",)),
        name="routed_experts",
    )(tile_expert, n_tiles, xs, w_gate, w_up, w_down)


def _combine_kernel(dest_ref, dnext_ref, wsel_ref, h_ref, sg_ref, su_ref, sd_ref, g_ref, b_ref, ys_hbm,
                    out_ref, yg_ref, mix_ref, sem, *, tc):
    i = pl.program_id(0)
    last = pl.num_programs(0) - 1

    def fetch(d_ref, buf, wait):
        def body(g, carry):
            r0 = g * DMA_GROUP
            slots = [[0 if wait else d_ref[(r0 + u) * ROUTE_COLS + k] for k in range(TOP_K)]
                     for u in range(DMA_GROUP)]
            for u in range(DMA_GROUP):
                for k in range(TOP_K):
                    cp = _row_copy(ys_hbm, slots[u][k], yg_ref.at[buf, k], (r0 + u) * ROW_TILE, sem.at[buf])
                    cp.wait() if wait else cp.start(priority=k % 2)
            return carry
        lax.fori_loop(0, tc // DMA_GROUP, body, 0)

    cur = i % 2

    @pl.when(i == 0)
    def _():
        fetch(dest_ref, 0, False)

    @pl.when(jnp.logical_and(i < last, cur == 0))
    def _():
        fetch(dnext_ref, 1, False)

    @pl.when(jnp.logical_and(i < last, cur == 1))
    def _():
        fetch(dnext_ref, 0, False)

    h = h_ref[...]
    hb = _bf(h)
    shared = _dot(_bf(_silu(_dot(hb, sg_ref[...])) * _dot(hb, su_ref[...])), sd_ref[...])

    def finish(buf):
        fetch(dest_ref, buf, True)

        def mix(r, carry):
            row = pl.multiple_of(r * ROW_TILE, ROW_TILE)
            t = wsel_ref[r * ROUTE_COLS] * yg_ref[buf, 0, pl.ds(row, ROW_TILE), :]
            for k in range(1, TOP_K):
                t = t + wsel_ref[r * ROUTE_COLS + k] * yg_ref[buf, k, pl.ds(row, ROW_TILE), :]
            mix_ref[pl.ds(row, ROW_TILE), :] = t
            return carry

        lax.fori_loop(0, tc, mix, 0, unroll=4)
        acc = ALPHA * h + shared + _from_row_tiles(mix_ref, tc)
        out_ref[...] = _layer_norm(acc, g_ref[...], b_ref[...])

    @pl.when(cur == 0)
    def _():
        finish(0)

    @pl.when(cur == 1)
    def _():
        finish(1)


def _combine(h2d, ys, dest_flat, wsel_flat, sg_bf, su_bf, sd_bf, ln_g, ln_b, tc):
    n = h2d.shape[0]
    nblk = n // tc
    const = lambda shape: pl.BlockSpec(shape, lambda i: (0,) * len(shape))
    idx = lambda f: pl.BlockSpec((tc * ROUTE_COLS,), f, memory_space=pltpu.SMEM)
    return pl.pallas_call(
        functools.partial(_combine_kernel, tc=tc),
        out_shape=jax.ShapeDtypeStruct((n, D_MODEL), F32),
        grid=(nblk,),
        in_specs=[idx(lambda i: (i,)), idx(lambda i: (jnp.minimum(i + 1, nblk - 1),)), idx(lambda i: (i,)),
                  pl.BlockSpec((tc, D_MODEL), lambda i: (i, 0)),
                  const((D_MODEL, D_EXPERT)), const((D_MODEL, D_EXPERT)), const((D_EXPERT, D_MODEL)),
                  const((1, D_MODEL)), const((1, D_MODEL)),
                  pl.BlockSpec(memory_space=pl.ANY)],
        out_specs=pl.BlockSpec((tc, D_MODEL), lambda i: (i, 0)),
        scratch_shapes=[pltpu.VMEM((2, TOP_K, tc * ROW_TILE, LANES), F32),
                        pltpu.VMEM((tc * ROW_TILE, LANES), F32), pltpu.SemaphoreType.DMA((2,))],
        compiler_params=_params(("arbitrary",)),
        name="route_combine",
    )(dest_flat, dest_flat, wsel_flat, h2d, sg_bf, su_bf, sd_bf, ln_g, ln_b, ys)


def _moe(h2d, ht, w, *, tm):
    n = h2d.shape[0]
    n_tiles_max = (n * TOP_K) // tm + N_EXPERTS
    eidx, wsel, counts = _router(h2d, w["w_router"], w["e_bias"], _row_tile(n, 512))
    cnt = counts[0, :N_EXPERTS].astype(jnp.int32)
    tiles = (cnt + tm - 1) // tm
    tile_end = jnp.cumsum(tiles)
    off = ((tile_end - tiles) * tm).astype(F32)
    tile_expert = jnp.minimum(
        jnp.sum(tile_end[None, :] <= jnp.arange(n_tiles_max, dtype=jnp.int32)[:, None], axis=1),
        N_EXPERTS - 1).astype(jnp.int32)
    dest = _slots(eidx, jnp.pad(off, (0, LANES - N_EXPERTS))[None, :], _row_tile(n, 256))
    dest_flat = dest.reshape(n * ROUTE_COLS)
    n_tiles = tile_end[N_EXPERTS - 1:]
    pad_info = jnp.stack([(tile_end - tiles) * tm + cnt, tiles * tm - cnt], axis=1).reshape(2 * N_EXPERTS)
    xs = _scatter_rows(ht, dest_flat, pad_info, n_tiles, n_tiles_max, tm, _row_tile(n, 128))
    ys = _experts(xs, tile_expert, n_tiles, w["w_gate"], w["w_up"], w["w_down"], tm)
    return _combine(h2d, ys, dest_flat, wsel.reshape(n * ROUTE_COLS), w["ws_gate"], w["ws_up"], w["ws_down"],
                    w["ln2_g"], w["ln2_b"], _row_tile(n, 128))


def _row_tile(n, cap):
    t = cap
    while n % t:
        t //= 2
    return t


def _trunk(x, s_hg, s_rw, shift, w, *, hg_cfg, rw_tb, moe_tm):
    B, T, _ = x.shape
    n = B * T
    x2d = x.reshape(n, D_MODEL)
    p_hg, p_rw = _proj(x2d, w["w_in"], _row_tile(n, 512))
    p_hg = p_hg.reshape(B, T, HG_COLS)
    p_rw = p_rw.reshape(B, T, RW_COLS)
    hg_out, hg_state = _hgrn(p_hg, s_hg, w["hg_lb"], w["hg_norm_g"], **hg_cfg)
    rw_out, rw_state = _rwkv(p_rw, shift, s_rw, w["rw_mu"], w["rw_w0"], w["rw_wa"], w["rw_a0"],
                             w["rw_g2"], w["rw_k_k"], w["rw_k_a"], w["rw_r_k"], w["rw_gn_g"],
                             w["rw_gn_b"], tb=rw_tb)
    h, ht = _outproj(x2d, hg_out.reshape(n, HG_WIDTH), rw_out.reshape(n, RW_WIDTH), w["w_out"],
                     w["ln1_g"], w["ln1_b"], _row_tile(n, 512))
    y = _moe(h, ht, w, tm=moe_tm)
    return y.reshape(B, T, D_MODEL), hg_state, rw_state, p_rw[:, T - 1, :]


def kernel(x_prompt, x_sample, state_hgrn, state_rwkv, state_shift, w_in, w_out, hg_lb, hg_norm_g, rw_mu, rw_w0, rw_w2, rw_a0, rw_a2, rw_g2, rw_k_k, rw_k_a, rw_r_k, rw_gn_g, rw_gn_b, ln1_g, ln1_b, ln2_g, ln2_b, w_router, e_bias, w_gate, w_up, w_down, ws_gate, ws_up, ws_down):
    l = 0
    zeros_lora = jnp.zeros((RW_DH, RW_WIDTH), F32)
    w = {
        "w_in": _bf(w_in[l]), "w_out": _bf(w_out[l]),
        "hg_lb": hg_lb, "hg_norm_g": hg_norm_g[l][None, :],
        "rw_mu": rw_mu[l][None, :], "rw_w0": rw_w0[l][None, :], "rw_a0": rw_a0[l][None, :],
        "rw_wa": _bf(jnp.concatenate([jnp.concatenate([rw_w2[l], zeros_lora], axis=1),
                                      jnp.concatenate([zeros_lora, rw_a2[l]], axis=1)], axis=0)),
        "rw_g2": _bf(rw_g2[l]),
        "rw_k_k": rw_k_k[l][None, :], "rw_k_a": rw_k_a[l][None, :],
        "rw_r_k": rw_r_k[l].reshape(1, RW_WIDTH),
        "rw_gn_g": rw_gn_g[l][None, :], "rw_gn_b": rw_gn_b[l][None, :],
        "ln1_g": ln1_g[l][None, :], "ln1_b": ln1_b[l][None, :],
        "ln2_g": ln2_g[l][None, :], "ln2_b": ln2_b[l][None, :],
        "w_router": jnp.pad(w_router[l], ((0, 0), (0, LANES - N_EXPERTS))),
        "e_bias": jnp.pad(e_bias[l], (0, LANES - N_EXPERTS))[None, :],
        "w_gate": w_gate[l], "w_up": w_up[l], "w_down": w_down[l],
        "ws_gate": _bf(ws_gate[l]), "ws_up": _bf(ws_up[l]), "ws_down": _bf(ws_down[l]),
    }
    Bp = x_prompt.shape[0]
    zero_hg = jnp.zeros((Bp, HG_HEADS, HG_DK, HG_DK), F32)
    zero_rw = jnp.zeros((Bp, RW_HEADS, RW_DH, RW_DH), F32)
    zero_sh = jnp.zeros((Bp, RW_COLS), F32)
    yp, hg_p, rw_p, sh_p = _trunk(x_prompt, zero_hg, zero_rw, zero_sh, w,
                                  hg_cfg=dict(Bb=1, Tb=128, C=64, R=16), rw_tb=64, moe_tm=512)
    ys, hg_s, rw_s, sh_s = _trunk(x_sample, state_hgrn[l], state_rwkv[l], state_shift[l], w,
                                  hg_cfg=dict(Bb=8, Tb=8, C=8, R=8), rw_tb=8, moe_tm=128)
    return (yp, ys, hg_p[None], rw_p[None], sh_p[None], hg_s[None], rw_s[None], sh_s[None])
```

```python
import functools

import jax
import jax.numpy as jnp
from jax import lax
from jax.experimental import pallas as pl
from jax.experimental.pallas import tpu as pltpu

F32 = jnp.float32
BF16 = jnp.bfloat16

D_MODEL = 1024
HG_HEADS = 4
HG_DK = 128
HG_WIDTH = 512
HG_COLS = 2048
RW_HEADS = 8
RW_DH = 64
RW_WIDTH = 512
RW_COLS = 1792
N_EXPERTS = 64
TOP_K = 6
N_GROUPS = 8
TOPK_GROUPS = 4
GROUP_SIZE = N_EXPERTS // N_GROUPS
D_EXPERT = 256
ROUTED_SCALE = 2.5
DEPTH = 1
ALPHA = (2.0 * DEPTH) ** 0.25
LN_EPS = 1e-5
HEAD_NORM_EPS = 1e-6
RW_GN_EPS = 64e-5

LANES = 128
VMEM_LIMIT = 56 * 1024 * 1024


def _bf(x):
    return x.astype(BF16)


def _dot(a, b):
    return jnp.dot(a, b, preferred_element_type=F32)


def _dot_nt(a, b):
    return lax.dot_general(a, b, (((1,), (1,)), ((), ())), preferred_element_type=F32)


def _split2(x):
    hi = _bf(x)
    lo = _bf(x - hi.astype(F32))
    return hi, lo


def _split3(x):
    hi = _bf(x)
    r1 = x - hi.astype(F32)
    mid = _bf(r1)
    lo = _bf(r1 - mid.astype(F32))
    return hi, mid, lo


def _sigmoid(x):
    return 1.0 / (1.0 + jnp.exp(-x))


def _silu(x):
    return x * _sigmoid(x)


def _params(sem):
    return pltpu.CompilerParams(dimension_semantics=sem, vmem_limit_bytes=VMEM_LIMIT)


def _proj_kernel(x_ref, w_ref, hg_ref, rw_ref):
    xb = _bf(x_ref[...])
    step = 256
    for c0 in range(0, HG_COLS, step):
        hg_ref[:, c0:c0 + step] = _dot(xb, w_ref[:, c0:c0 + step])
    for c0 in range(0, RW_COLS, step):
        rw_ref[:, c0:c0 + step] = _dot(xb, w_ref[:, HG_COLS + c0:HG_COLS + c0 + step])


def _proj(x2d, w_in_bf, tm):
    n = x2d.shape[0]
    return pl.pallas_call(
        _proj_kernel,
        out_shape=(jax.ShapeDtypeStruct((n, HG_COLS), F32), jax.ShapeDtypeStruct((n, RW_COLS), F32)),
        grid=(n // tm,),
        in_specs=[pl.BlockSpec((tm, D_MODEL), lambda i: (i, 0)),
                  pl.BlockSpec((D_MODEL, HG_COLS + RW_COLS), lambda i: (0, 0))],
        out_specs=(pl.BlockSpec((tm, HG_COLS), lambda i: (i, 0)),
                   pl.BlockSpec((tm, RW_COLS), lambda i: (i, 0))),
        compiler_params=_params(("parallel",)),
        name="in_proj",
    )(x2d, w_in_bf)


def _hgrn_kernel(p_ref, s0_ref, lbraw_ref, ng_ref, out_ref, sfin_ref, st_ref, *, Bb, Tb, C, R):
    ti = pl.program_id(1)
    nt = pl.num_programs(1)
    nb = C // R

    @pl.when(ti == 0)
    def _():
        for b in range(Bb):
            for h in range(HG_HEADS):
                st_ref[b, h] = s0_ref[b, h].T

    lbr = lbraw_ref[...]
    lbe = jnp.exp(lbr - jnp.max(lbr, axis=0, keepdims=True))
    lb = lbe[0:1, :] / jnp.sum(lbe, axis=0, keepdims=True)
    ng = ng_ref[...]

    if Tb > 8:
        rows = lax.broadcasted_iota(jnp.int32, (Tb, Tb), 0)
        cols = lax.broadcasted_iota(jnp.int32, (Tb, Tb), 1)
        tril = _bf(jnp.where(((rows // C) == (cols // C)) & (cols <= rows), 1.0, 0.0))
    tidx = lax.broadcasted_iota(jnp.int32, (R, 1), 0)
    cidx = lax.broadcasted_iota(jnp.int32, (C, 1), 0)

    for b in range(Bb):
        qs = _silu(p_ref[b, :, 0:512])
        f = lb + (1.0 - lb) * _sigmoid(p_ref[b, :, 512:1024])
        lf = jnp.log(f)
        kd = 1.0 - f
        v = p_ref[b, :, 1024:1536]
        if Tb == 8:
            r8 = lax.broadcasted_iota(jnp.int32, (8, 1), 0)
            bcum = jnp.concatenate(
                [jnp.sum(jnp.where(r8 <= t, lf, 0.0), axis=0, keepdims=True) for t in range(8)], axis=0)
        else:
            hi, mid, lo = _split3(lf)
            bcum = _dot(tril, hi) + _dot(tril, mid) + _dot(tril, lo)
        o_heads = []
        for h in range(HG_HEADS):
            sl = slice(h * HG_DK, (h + 1) * HG_DK)
            st = st_ref[b, h]
            o_chunks = []
            for c in range(Tb // C):
                r0 = c * C
                bc = bcum[r0:r0 + C, sl]
                qc = qs[r0:r0 + C, sl]
                kc = kd[r0:r0 + C, sl]
                vc = v[r0:r0 + C, sl]
                b_last = bc[C - 1:C, :]
                o_inter = _dot_nt(_bf(qc * jnp.exp(bc)), _bf(st))
                o_blocks = []
                for i in range(nb):
                    rs = slice(i * R, (i + 1) * R)
                    bq = bc[rs]
                    qq = qc[rs]
                    acc = jnp.zeros((R, HG_DK), F32)
                    if i > 0:
                        anchor = bc[i * R - 1:i * R, :]
                        early = cidx < i * R
                        qa = qq * jnp.exp(bq - anchor)
                        ka = jnp.where(early, kc * jnp.exp(jnp.where(early, anchor - bc, 0.0)), 0.0)
                        att = _dot_nt(_bf(qa), _bf(ka))
                        acc = acc + _dot(_bf(att), _bf(vc))
                    for s in range(R):
                        sa = i * R + s
                        msk = tidx >= s
                        e = jnp.exp(jnp.where(msk, bq - bc[sa:sa + 1, :], 0.0))
                        term = jnp.where(msk, qq * kc[sa:sa + 1, :] * e, 0.0)
                        acc = acc + jnp.sum(term, axis=-1, keepdims=True) * vc[sa:sa + 1, :]
                    o_blocks.append(acc)
                o_intra = o_blocks[0] if nb == 1 else jnp.concatenate(o_blocks, axis=0)
                o_chunks.append(o_inter + o_intra)
                k_out = kc * jnp.exp(b_last - bc)
                st = st * jnp.exp(b_last) + _dot(_bf(vc.T), _bf(k_out))
            st_ref[b, h] = st
            o_h = o_chunks[0] if len(o_chunks) == 1 else jnp.concatenate(o_chunks, axis=0)
            o_h = o_h * lax.rsqrt(jnp.mean(o_h * o_h, axis=-1, keepdims=True) + HEAD_NORM_EPS) * ng[:, sl]
            o_heads.append(o_h)
        out_ref[b] = jnp.concatenate(o_heads, axis=-1) * _silu(p_ref[b, :, 1536:2048])

    @pl.when(ti == nt - 1)
    def _():
        for b in range(Bb):
            for h in range(HG_HEADS):
                sfin_ref[b, h] = st_ref[b, h].T


def _hgrn(p_hg, s0, lb_raw, norm_g, *, Bb, Tb, C, R):
    B, T, _ = p_hg.shape
    kern = functools.partial(_hgrn_kernel, Bb=Bb, Tb=Tb, C=C, R=R)
    return pl.pallas_call(
        kern,
        out_shape=(jax.ShapeDtypeStruct((B, T, HG_WIDTH), F32),
                   jax.ShapeDtypeStruct((B, HG_HEADS, HG_DK, HG_DK), F32)),
        grid=(B // Bb, T // Tb),
        in_specs=[pl.BlockSpec((Bb, Tb, HG_COLS), lambda i, j: (i, j, 0)),
                  pl.BlockSpec((Bb, HG_HEADS, HG_DK, HG_DK), lambda i, j: (i, 0, 0, 0)),
                  pl.BlockSpec((DEPTH + 1, HG_WIDTH), lambda i, j: (0, 0)),
                  pl.BlockSpec((1, HG_WIDTH), lambda i, j: (0, 0))],
        out_specs=(pl.BlockSpec((Bb, Tb, HG_WIDTH), lambda i, j: (i, j, 0)),
                   pl.BlockSpec((Bb, HG_HEADS, HG_DK, HG_DK), lambda i, j: (i, 0, 0, 0))),
        scratch_shapes=[pltpu.VMEM((Bb, HG_HEADS, HG_DK, HG_DK), F32)],
        compiler_params=_params(("parallel", "arbitrary")),
        name="hgrn2_mixer",
    )(p_hg, s0, lb_raw, norm_g)


RW_PAIRS = RW_HEADS // 2
EXP_NEG_HALF = 0.6065306597126334
RW_SEQ = 8
RW_AHEAD = 6


def _block_ones():
    i = lax.broadcasted_iota(jnp.int32, (LANES, LANES), 0)
    j = lax.broadcasted_iota(jnp.int32, (LANES, LANES), 1)
    return _bf(jnp.where((i // RW_DH) == (j // RW_DH), 1.0, 0.0))


def _seg_sum(x, bd):
    outs = []
    for c in range(RW_WIDTH // LANES):
        hi, lo = _split2(x[:, c * LANES:(c + 1) * LANES])
        outs.append(_dot(hi, bd) + _dot(lo, bd))
    return jnp.concatenate(outs, axis=-1)


def _rwkv_kernel(p_ref, shift_ref, s0_ref, mu_ref, w0_ref, wa_ref, a0_ref, g2_ref, kk_ref, ka_ref,
                 rk_ref, gng_ref, gnb_ref, out_ref, sfin_ref,
                 s_ref, prev_ref, r_s, w_s, k_s, v_s, kk_s, kka_s, g_s, vt_s, o_s, *, tb):
    ti = pl.program_id(1)
    nt = pl.num_programs(1)
    pairs = range(RW_PAIRS)
    sub = 8

    @pl.when(ti == 0)
    def _():
        prev_ref[...] = shift_ref[...]
        for b in range(RW_SEQ):
            for p in pairs:
                s_ref[b, p * RW_DH:(p + 1) * RW_DH, :] = jnp.concatenate(
                    [s0_ref[b, 2 * p], s0_ref[b, 2 * p + 1]], axis=-1)

    bd = _block_ones()
    lane = lax.broadcasted_iota(jnp.int32, (1, LANES), 1)
    rid = lax.broadcasted_iota(jnp.int32, (tb, 1), 0)
    mu = mu_ref[...]

    for b in range(RW_SEQ):
        p = p_ref[b]
        prev = jnp.where(rid == 0, prev_ref[b:b + 1, :], pltpu.roll(p, 1, axis=0))
        prev_ref[b:b + 1, :] = p[tb - 1:tb, :]
        ps = p + mu * (prev - p)
        r = ps[:, 0:512]
        k = ps[:, 512:1024]
        v = ps[:, 1024:1536]
        zz = ps[:, 1536:1664]
        zg = ps[:, 1664:1792]
        lora = _dot(_bf(jnp.where(lane < 64, jnp.tanh(zz), zz)), wa_ref[...])
        exp_w = EXP_NEG_HALF / (1.0 + jnp.exp(-(w0_ref[...] + lora[:, 0:512])))
        a = _sigmoid(a0_ref[...] + lora[:, 512:1024])
        kk = k * kk_ref[...]
        kk = kk / jnp.maximum(jnp.sqrt(_seg_sum(kk * kk, bd)), 1e-12)
        r_s[b] = r
        w_s[b] = jnp.exp(-exp_w)
        k_s[b] = k * (1.0 + (a - 1.0) * ka_ref[...])
        v_s[b] = v
        kk_s[b] = kk
        kka_s[b] = kk * a
        g_s[b] = _dot(_bf(_sigmoid(zg)), g2_ref[...])
        for q in pairs:
            vp = v[:, q * LANES:(q + 1) * LANES]
            if tb < RW_DH:
                vp = jnp.concatenate([vp, jnp.zeros((RW_DH - tb, LANES), F32)], axis=0)
            vpt = vp.T
            vt_s[b, q * RW_DH:(q + 1) * RW_DH, :] = _bf(
                jnp.concatenate([vpt[0:RW_DH, :], vpt[RW_DH:LANES, :]], axis=1))

    ci = lax.broadcasted_iota(jnp.int32, (LANES, LANES), 0)
    cj = lax.broadcasted_iota(jnp.int32, (LANES, LANES), 1)
    same_head = (ci // RW_DH) == (cj // RW_DH)
    si = lax.broadcasted_iota(jnp.int32, (16, 2 * LANES), 0)
    sj = lax.broadcasted_iota(jnp.int32, (16, 2 * LANES), 1)
    head_rows = _bf(jnp.where(si == sj // RW_DH, 1.0, 0.0))

    def rows(ref, b, t0, j):
        return [ref[b, pl.ds(t0, sub), pl.ds(q * LANES, LANES)][j:j + 1, :] for q in pairs]

    def scale(x, rws):
        return jnp.concatenate([x[q * RW_DH:(q + 1) * RW_DH] * rws[q] for q in pairs], axis=0)

    def side_by_side(x):
        return jnp.concatenate(
            [jnp.concatenate([x[0:RW_DH], x[RW_DH:2 * RW_DH]], axis=1),
             jnp.concatenate([x[2 * RW_DH:3 * RW_DH], x[3 * RW_DH:4 * RW_DH]], axis=1)], axis=0)

    def steps(tq, carry):
        t0 = pl.multiple_of(tq * sub, sub)
        for j in range(sub):
            tl = tq * sub + j
            pick = _bf(jnp.where(same_head & ((ci % RW_DH) == tl), 1.0, 0.0))
            def reductions(b):
                return (_dot(_bf(scale(s_ref[b], rows(kk_s, b, t0, j))), bd), _dot(vt_s[b], pick))

            ahead = [reductions(b) for b in range(RW_AHEAD)]
            for b in range(RW_SEQ):
                sa, vb = ahead.pop(0)
                if b + RW_AHEAD < RW_SEQ:
                    ahead.append(reductions(b + RW_AHEAD))
                s = (scale(s_ref[b], rows(w_s, b, t0, j)) - scale(sa, rows(kka_s, b, t0, j))
                     + scale(vb, rows(k_s, b, t0, j)))
                s_ref[b] = s
                o4 = _dot_nt(head_rows, _bf(side_by_side(scale(s, rows(r_s, b, t0, j)))))
                for h in range(RW_PAIRS):
                    o_s[h, tq, j:j + 1, b * LANES:(b + 1) * LANES] = o4[h:h + 1, :]
        return carry

    lax.fori_loop(0, tb // sub, steps, 0)

    for b in range(RW_SEQ):
        lanes = slice(b * LANES, (b + 1) * LANES)
        nq = tb // sub
        oh = [jnp.concatenate([o_s[h, q, :, lanes] for q in range(nq)], axis=0) if nq > 1
              else o_s[h, 0, :, lanes] for h in range(RW_PAIRS)]
        o = jnp.concatenate([oh[h % RW_PAIRS][:, (h // RW_PAIRS) * RW_DH:(h // RW_PAIRS + 1) * RW_DH]
                             for h in range(RW_HEADS)], axis=-1)
        d = o - _seg_sum(o, bd) * (1.0 / RW_DH)
        var = _seg_sum(d * d, bd) * (1.0 / RW_DH)
        on = d * lax.rsqrt(var + RW_GN_EPS) * gng_ref[...] + gnb_ref[...]
        bonus = _seg_sum(r_s[b] * k_s[b] * rk_ref[...], bd) * v_s[b]
        out_ref[b] = (on + bonus) * g_s[b]

    @pl.when(ti == nt - 1)
    def _():
        for b in range(RW_SEQ):
            for p in pairs:
                s = s_ref[b, p * RW_DH:(p + 1) * RW_DH, :]
                sfin_ref[b, 2 * p] = s[:, 0:RW_DH]
                sfin_ref[b, 2 * p + 1] = s[:, RW_DH:LANES]


def _rwkv(p_rw, shift, s0, mu, w0, wa, a0, g2, k_k, k_a, r_k, gn_g, gn_b, *, tb):
    B, T, _ = p_rw.shape
    kern = functools.partial(_rwkv_kernel, tb=tb)
    vec = lambda n: pl.BlockSpec((1, n), lambda i, j: (0, 0))
    blk = pltpu.VMEM((RW_SEQ, tb, RW_WIDTH), F32)
    oblk = pltpu.VMEM((RW_PAIRS, tb // 8, 8, RW_SEQ * LANES), F32)
    return pl.pallas_call(
        kern,
        out_shape=(jax.ShapeDtypeStruct((B, T, RW_WIDTH), F32),
                   jax.ShapeDtypeStruct((B, RW_HEADS, RW_DH, RW_DH), F32)),
        grid=(B // RW_SEQ, T // tb),
        in_specs=[pl.BlockSpec((RW_SEQ, tb, RW_COLS), lambda i, j: (i, j, 0)),
                  pl.BlockSpec((RW_SEQ, RW_COLS), lambda i, j: (i, 0)),
                  pl.BlockSpec((RW_SEQ, RW_HEADS, RW_DH, RW_DH), lambda i, j: (i, 0, 0, 0)),
                  vec(RW_COLS), vec(RW_WIDTH),
                  pl.BlockSpec((LANES, 2 * RW_WIDTH), lambda i, j: (0, 0)),
                  vec(RW_WIDTH),
                  pl.BlockSpec((LANES, RW_WIDTH), lambda i, j: (0, 0)),
                  vec(RW_WIDTH), vec(RW_WIDTH), vec(RW_WIDTH), vec(RW_WIDTH), vec(RW_WIDTH)],
        out_specs=(pl.BlockSpec((RW_SEQ, tb, RW_WIDTH), lambda i, j: (i, j, 0)),
                   pl.BlockSpec((RW_SEQ, RW_HEADS, RW_DH, RW_DH), lambda i, j: (i, 0, 0, 0))),
        scratch_shapes=[pltpu.VMEM((RW_SEQ, RW_PAIRS * RW_DH, LANES), F32),
                        pltpu.VMEM((RW_SEQ, RW_COLS), F32),
                        blk, blk, blk, blk, blk, blk, blk,
                        pltpu.VMEM((RW_SEQ, RW_PAIRS * RW_DH, LANES), BF16),
                        oblk],
        compiler_params=_params(("parallel", "arbitrary")),
        name="rwkv7_mixer",
    )(p_rw, shift, s0, mu, w0, wa, a0, g2, k_k, k_a, r_k, gn_g, gn_b)


def _layer_norm(y, g, b):
    mu = jnp.mean(y, axis=-1, keepdims=True)
    d = y - mu
    var = jnp.mean(d * d, axis=-1, keepdims=True)
    return d * lax.rsqrt(var + LN_EPS) * g + b


ROW_TILE = 8


def _to_row_tiles(ref, x):
    rows = x.shape[0]
    for s in range(ROW_TILE):
        ref[pl.ds(s, rows, stride=ROW_TILE), :] = x[:, s * LANES:(s + 1) * LANES]


def _from_row_tiles(ref, rows):
    return jnp.concatenate([ref[pl.ds(s, rows, stride=ROW_TILE), :] for s in range(ROW_TILE)], axis=-1)


def _outproj_kernel(x_ref, hg_ref, rw_ref, w_ref, g_ref, b_ref, sg_ref, su_ref, sd_ref, h_ref, ht_ref, base_ref):
    mix = (_dot(_bf(hg_ref[...]), w_ref[0:HG_WIDTH, :])
           + _dot(_bf(rw_ref[...]), w_ref[HG_WIDTH:HG_WIDTH + RW_WIDTH, :]))
    h = _layer_norm(ALPHA * x_ref[...] + mix, g_ref[...], b_ref[...])
    h_ref[...] = h
    _to_row_tiles(ht_ref, h)
    hb = _bf(h)
    shared = _dot(_bf(_silu(_dot(hb, sg_ref[...])) * _dot(hb, su_ref[...])), sd_ref[...])
    base_ref[...] = ALPHA * h + shared


def _outproj(x2d, hg2d, rw2d, w_out_bf, ln_g, ln_b, sg_bf, su_bf, sd_bf, tm):
    n = x2d.shape[0]
    rows = lambda w: pl.BlockSpec((tm, w), lambda i: (i, 0))
    const = lambda shape: pl.BlockSpec(shape, lambda i: (0,) * len(shape))
    return pl.pallas_call(
        _outproj_kernel,
        out_shape=(jax.ShapeDtypeStruct((n, D_MODEL), F32),
                   jax.ShapeDtypeStruct((n * ROW_TILE, LANES), F32),
                   jax.ShapeDtypeStruct((n, D_MODEL), F32)),
        grid=(n // tm,),
        in_specs=[rows(D_MODEL), rows(HG_WIDTH), rows(RW_WIDTH),
                  const((D_MODEL, D_MODEL)), const((1, D_MODEL)), const((1, D_MODEL)),
                  const((D_MODEL, D_EXPERT)), const((D_MODEL, D_EXPERT)), const((D_EXPERT, D_MODEL))],
        out_specs=(rows(D_MODEL), pl.BlockSpec((tm * ROW_TILE, LANES), lambda i: (i, 0)), rows(D_MODEL)),
        compiler_params=_params(("parallel",)),
        name="out_proj_ln",
    )(x2d, hg2d, rw2d, w_out_bf, ln_g, ln_b, sg_bf, su_bf, sd_bf)


ROUTE_COLS = 8


def _router_kernel(h_ref, w_ref, eb_ref, eidx_ref, wsel_ref, cnt_ref):
    tm = h_ref.shape[0]
    hh, hl = _split2(h_ref[...])
    wh, wl = _split2(w_ref[...])
    logits = _dot(hh, wh) + (_dot(hh, wl) + _dot(hl, wh))
    lane = lax.broadcasted_iota(jnp.int32, (tm, LANES), 1)
    valid = lane < N_EXPERTS
    neg = -jnp.inf
    scores = _sigmoid(logits)
    choice = jnp.where(valid, scores + eb_ref[...], neg)

    def partner(x, s):
        return jnp.where((lane & s) != 0, pltpu.roll(x, s, axis=1), pltpu.roll(x, LANES - s, axis=1))

    def group_max(x):
        for s in (1, 2, 4):
            x = jnp.maximum(x, partner(x, s))
        return x

    def group_min(x):
        for s in (1, 2, 4):
            x = jnp.minimum(x, partner(x, s))
        return x

    m1 = group_max(choice)
    first = group_min(jnp.where(choice == m1, lane, LANES))
    m2 = group_max(jnp.where(lane == first, neg, choice))
    gscore = m1 + m2
    gper = jnp.where(valid, gscore, pltpu.roll(gscore, N_EXPERTS, axis=1))
    gidx = (lane % N_EXPERTS) // GROUP_SIZE
    rank = jnp.zeros((tm, LANES), jnp.int32)
    for j in range(1, N_GROUPS):
        other = pltpu.roll(gper, GROUP_SIZE * j, axis=1)
        beats = (other > gper) | ((other == gper) & (gidx >= j))
        rank = rank + beats.astype(jnp.int32)
    cand = jnp.where(valid & (rank < TOPK_GROUPS), choice, neg)
    sel = jnp.zeros((tm, LANES), jnp.bool_)
    hits = []
    eidx = jnp.zeros((tm, LANES), jnp.int32)
    for k in range(TOP_K):
        m = jnp.max(cand, axis=-1, keepdims=True)
        idx = jnp.min(jnp.where(cand == m, lane, LANES), axis=-1, keepdims=True)
        hit = lane == idx
        hits.append(hit)
        eidx = jnp.where(lane == k, idx, eidx)
        sel = sel | hit
        cand = jnp.where(hit, neg, cand)
    wts = jnp.where(sel, scores, 0.0)
    gates = wts / jnp.sum(wts, axis=-1, keepdims=True) * ROUTED_SCALE
    wsel = jnp.zeros((tm, LANES), F32)
    for k in range(TOP_K):
        wsel = jnp.where(lane == k, jnp.sum(jnp.where(hits[k], gates, 0.0), axis=-1, keepdims=True), wsel)
    eidx_ref[...] = eidx[:, 0:ROUTE_COLS]
    wsel_ref[...] = wsel[:, 0:ROUTE_COLS]

    @pl.when(pl.program_id(0) == 0)
    def _():
        cnt_ref[...] = jnp.zeros((1, LANES), F32)

    cnt_ref[...] += jnp.sum(jnp.where(sel, 1.0, 0.0), axis=0, keepdims=True)


def _router(h2d, w_router_pad, e_bias_pad, tm):
    n = h2d.shape[0]
    return pl.pallas_call(
        _router_kernel,
        out_shape=(jax.ShapeDtypeStruct((n, ROUTE_COLS), jnp.int32),
                   jax.ShapeDtypeStruct((n, ROUTE_COLS), F32),
                   jax.ShapeDtypeStruct((1, LANES), F32)),
        grid=(n // tm,),
        in_specs=[pl.BlockSpec((tm, D_MODEL), lambda i: (i, 0)),
                  pl.BlockSpec((D_MODEL, LANES), lambda i: (0, 0)),
                  pl.BlockSpec((1, LANES), lambda i: (0, 0))],
        out_specs=(pl.BlockSpec((tm, ROUTE_COLS), lambda i: (i, 0)),
                   pl.BlockSpec((tm, ROUTE_COLS), lambda i: (i, 0)),
                   pl.BlockSpec((1, LANES), lambda i: (0, 0))),
        compiler_params=_params(("arbitrary",)),
        name="router",
    )(h2d, w_router_pad, e_bias_pad)


def _slot_kernel(eidx_ref, off_ref, dest_ref, carry_ref):
    tb = eidx_ref.shape[0]

    @pl.when(pl.program_id(0) == 0)
    def _():
        carry_ref[...] = jnp.zeros((1, LANES), F32)

    lane = lax.broadcasted_iota(jnp.int32, (tb, LANES), 1)
    eidx = eidx_ref[...]
    onehot = [lane == eidx[:, k:k + 1] for k in range(TOP_K)]
    member = jnp.zeros((tb, LANES), F32)
    for k in range(TOP_K):
        member = member + jnp.where(onehot[k], 1.0, 0.0)
    ri = lax.broadcasted_iota(jnp.int32, (tb, tb), 0)
    rj = lax.broadcasted_iota(jnp.int32, (tb, tb), 1)
    before = _bf(jnp.where(rj < ri, 1.0, 0.0))
    slot = _dot(before, _bf(member)) + carry_ref[...] + off_ref[...]
    dest = jnp.zeros((tb, LANES), F32)
    for k in range(TOP_K):
        dest = jnp.where(lane == k, jnp.sum(jnp.where(onehot[k], slot, 0.0), axis=-1, keepdims=True), dest)
    dest_ref[...] = dest[:, 0:ROUTE_COLS].astype(jnp.int32) * ROW_TILE
    carry_ref[...] += jnp.sum(member, axis=0, keepdims=True)


def _slots(eidx, off, tb):
    n = eidx.shape[0]
    return pl.pallas_call(
        _slot_kernel,
        out_shape=jax.ShapeDtypeStruct((n, ROUTE_COLS), jnp.int32),
        grid=(n // tb,),
        in_specs=[pl.BlockSpec((tb, ROUTE_COLS), lambda i: (i, 0)),
                  pl.BlockSpec((1, LANES), lambda i: (0, 0))],
        out_specs=pl.BlockSpec((tb, ROUTE_COLS), lambda i: (i, 0)),
        scratch_shapes=[pltpu.VMEM((1, LANES), F32)],
        compiler_params=_params(("arbitrary",)),
        name="route_slots",
    )(eidx, off)


DMA_GROUP = 8


def _row_copy(src_ref, src_row, dst_ref, dst_row, sem):
    return pltpu.make_async_copy(src_ref.at[pl.ds(pl.multiple_of(src_row, ROW_TILE), ROW_TILE)],
                                 dst_ref.at[pl.ds(pl.multiple_of(dst_row, ROW_TILE), ROW_TILE)], sem)


def _scatter_kernel(pad_ref, nt_ref, dest_ref, ht_ref, xs_hbm, zero_ref, sem, zsem, *, tb, tm, n_tiles_max):
    @pl.when(pl.program_id(0) == 0)
    def _():
        zero_ref[...] = jnp.zeros(zero_ref.shape, F32)

        def piece(rows, slot, wait):
            cp = pltpu.make_async_copy(
                zero_ref.at[pl.ds(0, rows * ROW_TILE)],
                xs_hbm.at[pl.ds(pl.multiple_of(slot * ROW_TILE, ROW_TILE), rows * ROW_TILE)], zsem)
            cp.wait() if wait else cp.start()

        def fill(wait):
            def pad(e, carry):
                slot = pad_ref[2 * e]
                length = pad_ref[2 * e + 1]
                bit = tm // 2
                while bit >= 1:
                    @pl.when((length & bit) != 0)
                    def _(bit=bit, slot=slot):
                        piece(bit, slot, wait)
                    slot = slot + (length & bit)
                    bit //= 2
                return carry

            def unused(g, carry):
                piece(tm, g * tm, wait)
                return carry

            lax.fori_loop(0, N_EXPERTS, pad, 0)
            lax.fori_loop(nt_ref[0], n_tiles_max, unused, 0)

        fill(False)
        fill(True)

    def start(g, carry):
        r0 = g * DMA_GROUP
        slots = [[dest_ref[(r0 + u) * ROUTE_COLS + k] for k in range(TOP_K)] for u in range(DMA_GROUP)]
        for u in range(DMA_GROUP):
            for k in range(TOP_K):
                _row_copy(ht_ref, (r0 + u) * ROW_TILE, xs_hbm, slots[u][k], sem).start(priority=k % 2)
        return carry

    def wait(r, carry):
        for k in range(TOP_K):
            _row_copy(ht_ref, 0, xs_hbm, 0, sem).wait()
        return carry

    lax.fori_loop(0, tb // DMA_GROUP, start, 0)
    lax.fori_loop(0, tb, wait, 0)


def _scatter_rows(ht, dest_flat, pad_info, n_tiles, n_tiles_max, tm, tb):
    n = ht.shape[0] // ROW_TILE
    return pl.pallas_call(
        functools.partial(_scatter_kernel, tb=tb, tm=tm, n_tiles_max=n_tiles_max),
        out_shape=jax.ShapeDtypeStruct((n_tiles_max * tm * ROW_TILE, LANES), F32),
        grid_spec=pltpu.PrefetchScalarGridSpec(
            num_scalar_prefetch=2,
            grid=(n // tb,),
            in_specs=[pl.BlockSpec((tb * ROUTE_COLS,), lambda i, pad, nt: (i,), memory_space=pltpu.SMEM),
                      pl.BlockSpec((tb * ROW_TILE, LANES), lambda i, pad, nt: (i, 0))],
            out_specs=pl.BlockSpec(memory_space=pl.ANY),
            scratch_shapes=[pltpu.VMEM((tm * ROW_TILE, LANES), F32),
                            pltpu.SemaphoreType.DMA, pltpu.SemaphoreType.DMA]),
        compiler_params=_params(("arbitrary",)),
        name="route_scatter",
    )(pad_info, n_tiles, dest_flat, ht)


XS_RING = 3


def _experts_kernel(te_ref, nt_ref, xs_hbm, wg_ref, wu_ref, wd_ref, ys_ref, wgu_bf, wd_bf, xs_buf, xs_sem, *, tm):
    g = pl.program_id(0)
    n_live = nt_ref[0]
    live = g < n_live

    def tile_copy(t):
        return pltpu.make_async_copy(
            xs_hbm.at[pl.ds(pl.multiple_of(t * (tm * ROW_TILE), ROW_TILE), tm * ROW_TILE)],
            xs_buf.at[t % XS_RING], xs_sem.at[t % XS_RING])

    @pl.when(g == 0)
    def _():
        for t in range(XS_RING - 1):
            @pl.when(t < n_live)
            def _(t=t):
                tile_copy(t).start()

    @pl.when(g + (XS_RING - 1) < n_live)
    def _():
        tile_copy(g + (XS_RING - 1)).start()

    @pl.when(jnp.logical_or(g == 0, te_ref[g] != te_ref[jnp.maximum(g - 1, 0)]))
    def _():
        wgu_bf[:, 0:D_EXPERT] = _bf(wg_ref[0])
        wgu_bf[:, D_EXPERT:2 * D_EXPERT] = _bf(wu_ref[0])
        wd_bf[...] = _bf(wd_ref[0])

    @pl.when(live)
    def _():
        tile_copy(g).wait()
        xb = _bf(_from_row_tiles(xs_buf.at[g % XS_RING], tm))
        gu = _dot(xb, wgu_bf[...])
        act = _silu(gu[:, 0:D_EXPERT]) * gu[:, D_EXPERT:2 * D_EXPERT]
        _to_row_tiles(ys_ref, _dot(_bf(act), wd_bf[...]))

    @pl.when(jnp.logical_not(live))
    def _():
        ys_ref[...] = jnp.zeros(ys_ref.shape, F32)


def _experts(xs, tile_expert, n_tiles, w_gate, w_up, w_down, tm):
    n_slots = xs.shape[0] // ROW_TILE
    return pl.pallas_call(
        functools.partial(_experts_kernel, tm=tm),
        out_shape=jax.ShapeDtypeStruct((n_slots * ROW_TILE, LANES), F32),
        grid_spec=pltpu.PrefetchScalarGridSpec(
            num_scalar_prefetch=2,
            grid=(n_slots // tm,),
            in_specs=[pl.BlockSpec(memory_space=pl.ANY),
                      pl.BlockSpec((1, D_MODEL, D_EXPERT), lambda g, te, nt: (te[g], 0, 0)),
                      pl.BlockSpec((1, D_MODEL, D_EXPERT), lambda g, te, nt: (te[g], 0, 0)),
                      pl.BlockSpec((1, D_EXPERT, D_MODEL), lambda g, te, nt: (te[g], 0, 0))],
            out_specs=pl.BlockSpec((tm * ROW_TILE, LANES), lambda g, te, nt: (g, 0)),
            scratch_shapes=[pltpu.VMEM((D_MODEL, 2 * D_EXPERT), BF16), pltpu.VMEM((D_EXPERT, D_MODEL), BF16),
                            pltpu.VMEM((XS_RING, tm * ROW_TILE, LANES), F32),
                            pltpu.SemaphoreType.DMA((XS_RING,))]),
        compiler_params=_params(("arbitrary",)),
        name="routed_experts",
    )(tile_expert, n_tiles, xs, w_gate, w_up, w_down)


def _combine_kernel(dest_ref, dnext_ref, wsel_ref, base_ref, g_ref, b_ref, ys_hbm,
                    out_ref, yg_ref, mix_ref, sem, *, tc):
    i = pl.program_id(0)
    last = pl.num_programs(0) - 1

    def fetch(d_ref, buf, wait):
        def body(g, carry):
            r0 = g * DMA_GROUP
            slots = [[0 if wait else d_ref[(r0 + u) * ROUTE_COLS + k] for k in range(TOP_K)]
                     for u in range(DMA_GROUP)]
            for u in range(DMA_GROUP):
                for k in range(TOP_K):
                    cp = _row_copy(ys_hbm, slots[u][k], yg_ref.at[buf, k], (r0 + u) * ROW_TILE, sem.at[buf])
                    cp.wait() if wait else cp.start(priority=k % 2)
            return carry
        lax.fori_loop(0, tc // DMA_GROUP, body, 0)

    cur = i % 2

    @pl.when(i == 0)
    def _():
        fetch(dest_ref, 0, False)

    @pl.when(jnp.logical_and(i < last, cur == 0))
    def _():
        fetch(dnext_ref, 1, False)

    @pl.when(jnp.logical_and(i < last, cur == 1))
    def _():
        fetch(dnext_ref, 0, False)

    def finish(buf):
        fetch(dest_ref, buf, True)

        def mix(r, carry):
            row = pl.multiple_of(r * ROW_TILE, ROW_TILE)
            t = wsel_ref[r * ROUTE_COLS] * yg_ref[buf, 0, pl.ds(row, ROW_TILE), :]
            for k in range(1, TOP_K):
                t = t + wsel_ref[r * ROUTE_COLS + k] * yg_ref[buf, k, pl.ds(row, ROW_TILE), :]
            mix_ref[pl.ds(row, ROW_TILE), :] = t
            return carry

        lax.fori_loop(0, tc, mix, 0, unroll=4)
        out_ref[...] = _layer_norm(base_ref[...] + _from_row_tiles(mix_ref, tc), g_ref[...], b_ref[...])

    @pl.when(cur == 0)
    def _():
        finish(0)

    @pl.when(cur == 1)
    def _():
        finish(1)


def _combine(base, ys, dest_flat, wsel_flat, ln_g, ln_b, tc):
    n = base.shape[0]
    nblk = n // tc
    const = lambda shape: pl.BlockSpec(shape, lambda i: (0,) * len(shape))
    idx = lambda f: pl.BlockSpec((tc * ROUTE_COLS,), f, memory_space=pltpu.SMEM)
    return pl.pallas_call(
        functools.partial(_combine_kernel, tc=tc),
        out_shape=jax.ShapeDtypeStruct((n, D_MODEL), F32),
        grid=(nblk,),
        in_specs=[idx(lambda i: (i,)), idx(lambda i: (jnp.minimum(i + 1, nblk - 1),)), idx(lambda i: (i,)),
                  pl.BlockSpec((tc, D_MODEL), lambda i: (i, 0)),
                  const((1, D_MODEL)), const((1, D_MODEL)),
                  pl.BlockSpec(memory_space=pl.ANY)],
        out_specs=pl.BlockSpec((tc, D_MODEL), lambda i: (i, 0)),
        scratch_shapes=[pltpu.VMEM((2, TOP_K, tc * ROW_TILE, LANES), F32),
                        pltpu.VMEM((tc * ROW_TILE, LANES), F32), pltpu.SemaphoreType.DMA((2,))],
        compiler_params=_params(("arbitrary",)),
        name="route_combine",
    )(dest_flat, dest_flat, wsel_flat, base, ln_g, ln_b, ys)


def _moe(h2d, ht, base, w, *, tm):
    n = h2d.shape[0]
    n_tiles_max = (n * TOP_K) // tm + N_EXPERTS
    eidx, wsel, counts = _router(h2d, w["w_router"], w["e_bias"], _row_tile(n, 512))
    cnt = counts[0, :N_EXPERTS].astype(jnp.int32)
    tiles = (cnt + tm - 1) // tm
    tile_end = jnp.cumsum(tiles)
    off = ((tile_end - tiles) * tm).astype(F32)
    tile_expert = jnp.minimum(
        jnp.sum(tile_end[None, :] <= jnp.arange(n_tiles_max, dtype=jnp.int32)[:, None], axis=1),
        N_EXPERTS - 1).astype(jnp.int32)
    dest = _slots(eidx, jnp.pad(off, (0, LANES - N_EXPERTS))[None, :], _row_tile(n, 256))
    dest_flat = dest.reshape(n * ROUTE_COLS)
    n_tiles = tile_end[N_EXPERTS - 1:]
    pad_info = jnp.stack([(tile_end - tiles) * tm + cnt, tiles * tm - cnt], axis=1).reshape(2 * N_EXPERTS)
    xs = _scatter_rows(ht, dest_flat, pad_info, n_tiles, n_tiles_max, tm, _row_tile(n, 256))
    ys = _experts(xs, tile_expert, n_tiles, w["w_gate"], w["w_up"], w["w_down"], tm)
    return _combine(base, ys, dest_flat, wsel.reshape(n * ROUTE_COLS), w["ln2_g"], w["ln2_b"], _row_tile(n, 256))


def _row_tile(n, cap):
    t = cap
    while n % t:
        t //= 2
    return t


def _trunk(x, s_hg, s_rw, shift, w, *, hg_cfg, rw_tb, moe_tm):
    B, T, _ = x.shape
    n = B * T
    x2d = x.reshape(n, D_MODEL)
    p_hg, p_rw = _proj(x2d, w["w_in"], _row_tile(n, 512))
    p_hg = p_hg.reshape(B, T, HG_COLS)
    p_rw = p_rw.reshape(B, T, RW_COLS)
    hg_out, hg_state = _hgrn(p_hg, s_hg, w["hg_lb"], w["hg_norm_g"], **hg_cfg)
    rw_out, rw_state = _rwkv(p_rw, shift, s_rw, w["rw_mu"], w["rw_w0"], w["rw_wa"], w["rw_a0"],
                             w["rw_g2"], w["rw_k_k"], w["rw_k_a"], w["rw_r_k"], w["rw_gn_g"],
                             w["rw_gn_b"], tb=rw_tb)
    h, ht, base = _outproj(x2d, hg_out.reshape(n, HG_WIDTH), rw_out.reshape(n, RW_WIDTH), w["w_out"],
                           w["ln1_g"], w["ln1_b"], w["ws_gate"], w["ws_up"], w["ws_down"], _row_tile(n, 512))
    y = _moe(h, ht, base, w, tm=moe_tm)
    return y.reshape(B, T, D_MODEL), hg_state, rw_state, p_rw[:, T - 1, :]


def kernel(x_prompt, x_sample, state_hgrn, state_rwkv, state_shift, w_in, w_out, hg_lb, hg_norm_g, rw_mu, rw_w0, rw_w2, rw_a0, rw_a2, rw_g2, rw_k_k, rw_k_a, rw_r_k, rw_gn_g, rw_gn_b, ln1_g, ln1_b, ln2_g, ln2_b, w_router, e_bias, w_gate, w_up, w_down, ws_gate, ws_up, ws_down):
    l = 0
    zeros_lora = jnp.zeros((RW_DH, RW_WIDTH), F32)
    w = {
        "w_in": _bf(w_in[l]), "w_out": _bf(w_out[l]),
        "hg_lb": hg_lb, "hg_norm_g": hg_norm_g[l][None, :],
        "rw_mu": rw_mu[l][None, :], "rw_w0": rw_w0[l][None, :], "rw_a0": rw_a0[l][None, :],
        "rw_wa": _bf(jnp.concatenate([jnp.concatenate([rw_w2[l], zeros_lora], axis=1),
                                      jnp.concatenate([zeros_lora, rw_a2[l]], axis=1)], axis=0)),
        "rw_g2": _bf(rw_g2[l]),
        "rw_k_k": rw_k_k[l][None, :], "rw_k_a": rw_k_a[l][None, :],
        "rw_r_k": rw_r_k[l].reshape(1, RW_WIDTH),
        "rw_gn_g": rw_gn_g[l][None, :], "rw_gn_b": rw_gn_b[l][None, :],
        "ln1_g": ln1_g[l][None, :], "ln1_b": ln1_b[l][None, :],
        "ln2_g": ln2_g[l][None, :], "ln2_b": ln2_b[l][None, :],
        "w_router": jnp.pad(w_router[l], ((0, 0), (0, LANES - N_EXPERTS))),
        "e_bias": jnp.pad(e_bias[l], (0, LANES - N_EXPERTS))[None, :],
        "w_gate": w_gate[l], "w_up": w_up[l], "w_down": w_down[l],
        "ws_gate": _bf(ws_gate[l]), "ws_up": _bf(ws_up[l]), "ws_down": _bf(ws_down[l]),
    }
    Bp = x_prompt.shape[0]
    zero_hg = jnp.zeros((Bp, HG_HEADS, HG_DK, HG_DK), F32)
    zero_rw = jnp.zeros((Bp, RW_HEADS, RW_DH, RW_DH), F32)
    zero_sh = jnp.zeros((Bp, RW_COLS), F32)
    yp, hg_p, rw_p, sh_p = _trunk(x_prompt, zero_hg, zero_rw, zero_sh, w,
                                  hg_cfg=dict(Bb=1, Tb=128, C=64, R=16), rw_tb=64, moe_tm=512)
    ys, hg_s, rw_s, sh_s = _trunk(x_sample, state_hgrn[l], state_rwkv[l], state_shift[l], w,
                                  hg_cfg=dict(Bb=8, Tb=8, C=8, R=8), rw_tb=8, moe_tm=128)
    return (yp, ys, hg_p[None], rw_p[None], sh_p[None], hg_s[None], rw_s[None], sh_s[None])
```
